```python
import math
import jax, jax.numpy as jnp
from jax import lax
import numpy as np

D_MODEL = 1024
BATCH = 8
SEQ = 4096
DEPTH = 1

EPS = 1e-6
Q_BLOCK = 128
H_A = 8
QK_NOPE = 128
QK_ROPE = 64
V_DIM = 128
Q_LORA = 256
KV_LORA = 128
ROPE_THETA = 10000.0
H_B = 16
KV_B = 4
GROUP = H_B // KV_B
HD_B = 64
WINDOW = 128
NUM_BUCKETS = 32
MAX_DISTANCE = 128
D_FF = 2816
CONV_W = 3

W_IN_SIZES = (Q_LORA, KV_LORA + QK_ROPE, H_B * HD_B, KV_B * HD_B, KV_B * HD_B, D_MODEL, D_MODEL)
W_IN_COLS = sum(W_IN_SIZES)
W_IN_SPLITS = tuple(int(s) for s in np.cumsum(W_IN_SIZES)[:-1])

kernel_name = "hybrid_mla_swa_gated_convffn_encoder"


def rms_norm(x, g):
    xf = x.astype(jnp.float32)
    y = xf * lax.rsqrt(jnp.mean(xf * xf, axis=-1, keepdims=True) + EPS)
    return (y * g.astype(jnp.float32)).astype(x.dtype)


def apply_rope(x, positions):
    half = QK_ROPE // 2
    inv_freq = ROPE_THETA ** (-jnp.arange(half, dtype=jnp.float32) / half)
    ang = positions.astype(jnp.float32)[:, None] * inv_freq[None, :]
    cos = jnp.cos(ang)[None, :, None, :]
    sin = jnp.sin(ang)[None, :, None, :]
    xf = x.astype(jnp.float32)
    x1, x2 = xf[..., :half], xf[..., half:]
    out = jnp.concatenate([x1 * cos - x2 * sin, x2 * cos + x1 * sin], axis=-1)
    return out.astype(x.dtype)


def t5_bucket(rel):
    nb = NUM_BUCKETS // 2
    max_exact = nb // 2
    base = (rel > 0).astype(jnp.int32) * nb
    n = jnp.abs(rel)
    nf = jnp.maximum(n, 1).astype(jnp.float32)
    large = max_exact + (jnp.log(nf / max_exact) / math.log(MAX_DISTANCE / max_exact)
                         * (nb - max_exact)).astype(jnp.int32)
    large = jnp.minimum(large, nb - 1)
    return base + jnp.where(n < max_exact, n, large)


def mla_branch(q_lat, kv_lat, positions, q_a_norm_g, w_q_b, kv_a_norm_g, w_kv_b):
    B, S, _ = q_lat.shape
    q = (rms_norm(q_lat, q_a_norm_g) @ w_q_b).reshape(B, S, H_A, QK_NOPE + QK_ROPE)
    q = jnp.concatenate([q[..., :QK_NOPE], apply_rope(q[..., QK_NOPE:], positions)], axis=-1)
    c_kv, k_rope = kv_lat[..., :KV_LORA], kv_lat[..., KV_LORA:]
    kv = (rms_norm(c_kv, kv_a_norm_g) @ w_kv_b).reshape(B, S, H_A, QK_NOPE + V_DIM)
    k_nope, v = kv[..., :QK_NOPE], kv[..., QK_NOPE:]
    k_rope = apply_rope(k_rope[:, :, None, :], positions)
    k = jnp.concatenate([k_nope, jnp.broadcast_to(k_rope, (B, S, H_A, QK_ROPE))], axis=-1)
    scale = 1.0 / math.sqrt(QK_NOPE + QK_ROPE)
    nblk = S // Q_BLOCK
    qb = q.reshape(B, nblk, Q_BLOCK, H_A, QK_NOPE + QK_ROPE).transpose(1, 0, 2, 3, 4)

    def attend(q_blk):
        s = jnp.einsum('bqhd,bkhd->bhqk', q_blk, k, preferred_element_type=jnp.float32) * scale
        p = jax.nn.softmax(s, axis=-1).astype(v.dtype)
        return jnp.einsum('bhqk,bkhd->bqhd', p, v)

    o = lax.map(attend, qb)
    return o.transpose(1, 0, 2, 3, 4).reshape(B, S, H_A * V_DIM)


def window_branch(q, k, v, rel_bias, sinks):
    B, S, _ = q.shape
    q = q.reshape(B, S, KV_B, GROUP, HD_B)
    k = k.reshape(B, S, KV_B, HD_B)
    v = v.reshape(B, S, KV_B, HD_B)
    pad = ((0, 0), (WINDOW, WINDOW), (0, 0), (0, 0))
    kp = jnp.pad(k, pad)
    vp = jnp.pad(v, pad)
    span = Q_BLOCK + 2 * WINDOW
    a = jnp.arange(Q_BLOCK, dtype=jnp.int32)[:, None]
    c = jnp.arange(span, dtype=jnp.int32)[None, :]
    rel = c - WINDOW - a
    in_band = jnp.abs(rel) <= WINDOW
    bias = rel_bias[t5_bucket(rel)].astype(jnp.float32)
    bias = bias.transpose(2, 0, 1).reshape(KV_B, GROUP, Q_BLOCK, span)
    sink = sinks.astype(jnp.float32).reshape(1, KV_B, GROUP, 1, 1)
    scale = 1.0 / math.sqrt(HD_B)
    nblk = S // Q_BLOCK
    qb = q.reshape(B, nblk, Q_BLOCK, KV_B, GROUP, HD_B).transpose(1, 0, 2, 3, 4, 5)

    def attend(args):
        q_blk, n = args
        start = n * Q_BLOCK
        k_blk = lax.dynamic_slice_in_dim(kp, start, span, axis=1)
        v_blk = lax.dynamic_slice_in_dim(vp, start, span, axis=1)
        key_pos = start - WINDOW + c
        valid = in_band & (key_pos >= 0) & (key_pos < S)
        s = jnp.einsum('bqhgd,bkhd->bhgqk', q_blk, k_blk,
                       preferred_element_type=jnp.float32) * scale + bias
        s = jnp.where(valid, s, -1e30)
        sink_col = jnp.broadcast_to(sink, (B, KV_B, GROUP, Q_BLOCK, 1))
        p = jax.nn.softmax(jnp.concatenate([s, sink_col], axis=-1), axis=-1)[..., :span]
        return jnp.einsum('bhgqk,bkhd->bqhgd', p.astype(v_blk.dtype), v_blk)

    o = lax.map(attend, (qb, jnp.arange(nblk, dtype=jnp.int32)))
    return o.transpose(1, 0, 2, 3, 4, 5).reshape(B, S, H_B * HD_B)


def conv_ffn(h, w_up, conv_w, conv_b, w_down):
    u = h @ w_up
    up = jnp.pad(u, ((0, 0), (1, 1), (0, 0)))
    u = up[:, :-2] * conv_w[0] + up[:, 1:-1] * conv_w[1] + up[:, 2:] * conv_w[2] + conv_b
    g, val = u[..., :D_FF], u[..., D_FF:]
    return (jax.nn.silu(g) * val) @ w_down


def setup_inputs(seed: int = 0) -> dict:
    key = jax.random.key(seed)
    ks = jax.random.split(key, 20)
    f32 = jnp.float32
    L = DEPTH

    def nrm(k, shape, scale):
        return jax.random.normal(k, shape, f32) * scale

    def gain(k, shape):
        return 1.0 + 0.05 * jax.random.normal(k, shape, f32)

    return {
        "x": jax.random.normal(ks[0], (BATCH, SEQ, D_MODEL), f32),
        "positions": jnp.arange(SEQ, dtype=jnp.int32),
        "norm1_g": gain(ks[1], (L, D_MODEL)),
        "w_in": nrm(ks[2], (L, D_MODEL, W_IN_COLS), D_MODEL ** -0.5),
        "q_a_norm_g": gain(ks[3], (L, Q_LORA)),
        "w_q_b": nrm(ks[4], (L, Q_LORA, H_A * (QK_NOPE + QK_ROPE)), Q_LORA ** -0.5),
        "kv_a_norm_g": gain(ks[5], (L, KV_LORA)),
        "w_kv_b": nrm(ks[6], (L, KV_LORA, H_A * (QK_NOPE + V_DIM)), KV_LORA ** -0.5),
        "rel_bias": nrm(ks[7], (NUM_BUCKETS, H_B), 0.5),
        "sinks": nrm(ks[8], (L, H_B), 0.5),
        "w_out": nrm(ks[9], (L, D_MODEL, D_MODEL), D_MODEL ** -0.5),
        "norm2_g": gain(ks[10], (L, D_MODEL)),
        "w_up": nrm(ks[11], (L, D_MODEL, 2 * D_FF), D_MODEL ** -0.5),
        "conv_w": nrm(ks[12], (L, CONV_W, 2 * D_FF), CONV_W ** -0.5),
        "conv_b": nrm(ks[13], (L, 2 * D_FF), 0.02),
        "w_down": nrm(ks[14], (L, D_FF, D_MODEL), D_FF ** -0.5),
        "final_norm_g": gain(ks[15], (D_MODEL,)),
    }


def reference(x, positions, norm1_g, w_in, q_a_norm_g, w_q_b, kv_a_norm_g, w_kv_b, rel_bias,
              sinks, w_out, norm2_g, w_up, conv_w, conv_b, w_down, final_norm_g):
    for l in range(DEPTH):
        h = rms_norm(x, norm1_g[l])
        proj = h @ w_in[l]
        q_lat, kv_lat, q_b, k_b, v_b, gate_a, gate_b = jnp.split(proj, W_IN_SPLITS, axis=-1)
        o_a = mla_branch(q_lat, kv_lat, positions, q_a_norm_g[l], w_q_b[l],
                         kv_a_norm_g[l], w_kv_b[l])
        o_b = window_branch(q_b, k_b, v_b, rel_bias, sinks[l])
        mixed = jax.nn.sigmoid(gate_a) * o_a + jax.nn.sigmoid(gate_b) * o_b
        x = x + mixed @ w_out[l]
        x = x + conv_ffn(rms_norm(x, norm2_g[l]), w_up[l], conv_w[l], conv_b[l], w_down[l])
    return rms_norm(x, final_norm_g)
```

```python
import functools
import math

import jax
import jax.numpy as jnp
import numpy as np
from jax import lax
from jax.experimental import pallas as pl
from jax.experimental.pallas import tpu as pltpu

D_MODEL = 1024
EPS = 1e-6
Q_BLOCK = 128
H_A = 8
QK_NOPE = 128
QK_ROPE = 64
V_DIM = 128
Q_LORA = 256
KV_LORA = 128
ROPE_THETA = 10000.0
H_B = 16
KV_B = 4
GROUP = H_B // KV_B
HD_B = 64
WINDOW = 128
NUM_BUCKETS = 32
MAX_DISTANCE = 128
D_FF = 2816
SPAN = Q_BLOCK + 2 * WINDOW

LANES = 128
QK_PAD = 256
NEG_INF = -1e30
VMEM_LIMIT = 56 * 1024 * 1024

TM_PROJ = 512
TQ_MLA = 256
TM_OUT = 512
TM_FFN = 512
FF_CHUNK = 256
HALO = 16


def _const_spec(shape):
    nd = len(shape)
    return pl.BlockSpec(shape, lambda *_: (0,) * nd, pipeline_mode=pl.Buffered(1))


def _params(sem):
    return pltpu.CompilerParams(dimension_semantics=sem, vmem_limit_bytes=VMEM_LIMIT)


def _rms(x, g):
    return x * lax.rsqrt(jnp.mean(x * x, axis=-1, keepdims=True) + EPS) * g


def _swap_halves_32(x):
    lane = lax.broadcasted_iota(jnp.int32, x.shape, 1)
    up = pltpu.roll(x, LANES - 32, 1)
    down = pltpu.roll(x, 32, 1)
    return jnp.where((lane & 32) == 0, up, down)


def _rope_table_kernel(pos_ref, inv_ref, sign_ref, cos_ref, sin_ref):
    ang = pos_ref[...] * inv_ref[...]
    cos_ref[...] = jnp.cos(ang)
    sin_ref[...] = jnp.sin(ang) * sign_ref[...]


def _rope_tables(positions):
    S = positions.shape[0]
    half = QK_ROPE // 2
    inv_freq = ROPE_THETA ** (-jnp.arange(half, dtype=jnp.float32) / half)
    inv = jnp.tile(inv_freq, LANES // half)[None, :]
    sign = jnp.tile(jnp.concatenate([-jnp.ones(half, jnp.float32), jnp.ones(half, jnp.float32)]),
                    LANES // QK_ROPE)[None, :]
    pos = positions.astype(jnp.float32)[:, None]
    ts = 512
    return pl.pallas_call(
        _rope_table_kernel,
        grid=(S // ts,),
        in_specs=[pl.BlockSpec((ts, 1), lambda i: (i, 0)),
                  pl.BlockSpec((1, LANES), lambda i: (0, 0)),
                  pl.BlockSpec((1, LANES), lambda i: (0, 0))],
        out_specs=[pl.BlockSpec((ts, LANES), lambda i: (i, 0)),
                   pl.BlockSpec((ts, LANES), lambda i: (i, 0))],
        out_shape=[jax.ShapeDtypeStruct((S, LANES), jnp.float32)] * 2,
        compiler_params=_params(("parallel",)),
        name="rope_tables",
    )(pos, inv, sign)


def _t5_bucket(rel):
    nb = NUM_BUCKETS // 2
    max_exact = nb // 2
    base = (rel > 0).astype(jnp.int32) * nb
    n = jnp.abs(rel)
    nf = jnp.maximum(n, 1).astype(jnp.float32)
    large = max_exact + (jnp.log(nf / max_exact) / math.log(MAX_DISTANCE / max_exact)
                         * (nb - max_exact)).astype(jnp.int32)
    large = jnp.minimum(large, nb - 1)
    return base + jnp.where(n < max_exact, n, large)


def _bias_kernel(rel_bias_ref, bucket_ref, band_ref, out_ref):
    h = pl.program_id(0)
    bucket = bucket_ref[...]
    acc = jnp.zeros(bucket.shape, jnp.float32)
    for b in range(NUM_BUCKETS):
        acc = jnp.where(bucket == b, rel_bias_ref[b, h], acc)
    out_ref[0] = jnp.where(band_ref[...] != 0, acc, NEG_INF)


def _window_bias(rel_bias):
    a = jnp.arange(Q_BLOCK, dtype=jnp.int32)[:, None]
    c = jnp.arange(SPAN, dtype=jnp.int32)[None, :]
    rel = c - WINDOW - a
    bucket = _t5_bucket(rel)
    band = (jnp.abs(rel) <= WINDOW).astype(jnp.int32)
    return pl.pallas_call(
        _bias_kernel,
        grid=(H_B,),
        in_specs=[pl.BlockSpec(memory_space=pltpu.SMEM),
                  pl.BlockSpec((Q_BLOCK, SPAN), lambda h: (0, 0)),
                  pl.BlockSpec((Q_BLOCK, SPAN), lambda h: (0, 0))],
        out_specs=pl.BlockSpec((1, Q_BLOCK, SPAN), lambda h: (h, 0, 0)),
        out_shape=jax.ShapeDtypeStruct((H_B, Q_BLOCK, SPAN), jnp.float32),
        compiler_params=_params(("arbitrary",)),
        name="window_bias",
    )(rel_bias.astype(jnp.float32), bucket, band)


def _proj_kernel(x_ref, g1_ref, w_in_ref, gq_ref, wq_ref, gkv_ref, wkv_ref, cos_ref, sin_ref,
                 qa_ref, ka_ref, va_ref, qb_ref, kb_ref, vb_ref, gate_ref, *, scale_a):
    x = x_ref[...]
    h = _rms(x, g1_ref[...]).astype(jnp.bfloat16)
    cos = cos_ref[...]
    sin = sin_ref[...]

    def rope(blk):
        return blk * cos + _swap_halves_32(blk) * sin

    lat = jnp.dot(h, w_in_ref[:, 0:512], preferred_element_type=jnp.float32)
    q_lat = lat[:, 0:Q_LORA]
    c_kv = lat[:, Q_LORA:Q_LORA + KV_LORA]
    k_rope = rope(lat[:, 384:512]).astype(jnp.bfloat16)

    qn = _rms(q_lat, gq_ref[...]).astype(jnp.bfloat16)
    q = jnp.dot(qn, wq_ref[...], preferred_element_type=jnp.float32)
    for hd in range(H_A):
        base = hd * QK_PAD
        qa_ref[:, base:base + QK_NOPE] = (q[:, base:base + QK_NOPE] * scale_a).astype(jnp.bfloat16)
        qa_ref[:, base + QK_NOPE:base + QK_PAD] = (
            rope(q[:, base + QK_NOPE:base + QK_PAD]) * scale_a).astype(jnp.bfloat16)

    cn = _rms(c_kv, gkv_ref[...]).astype(jnp.bfloat16)
    kv = jnp.dot(cn, wkv_ref[...], preferred_element_type=jnp.float32)
    for hd in range(H_A):
        base = hd * QK_PAD
        ka_ref[:, base:base + QK_NOPE] = kv[:, hd * QK_NOPE:(hd + 1) * QK_NOPE].astype(jnp.bfloat16)
        ka_ref[:, base + QK_NOPE:base + QK_PAD] = k_rope
    va_ref[...] = kv[:, H_A * QK_NOPE:].astype(jnp.bfloat16)

    qkv_b = jnp.dot(h, w_in_ref[:, 512:2048], preferred_element_type=jnp.float32)
    qb_ref[...] = qkv_b[:, 0:1024].astype(jnp.bfloat16)
    kb_ref[...] = qkv_b[:, 1024:1280].astype(jnp.bfloat16)
    vb_ref[...] = qkv_b[:, 1280:1536].astype(jnp.bfloat16)

    gate_ref[...] = jnp.dot(h, w_in_ref[:, 2048:4096], preferred_element_type=jnp.float32)


def _input_projection(x2d, norm1_g, w_in_p, gq, wq_p, gkv, wkv_p, cos, sin, S):
    N = x2d.shape[0]
    tm = TM_PROJ
    s_tiles = S // tm
    row = lambda i: (i, 0)
    tab = lambda i: (i % s_tiles, 0)
    bf = jnp.bfloat16
    outs = [(H_A * QK_PAD, bf), (H_A * QK_PAD, bf), (H_A * V_DIM, bf),
            (H_B * HD_B, bf), (KV_B * HD_B, bf), (KV_B * HD_B, bf), (2 * D_MODEL, jnp.float32)]
    return pl.pallas_call(
        functools.partial(_proj_kernel, scale_a=1.0 / math.sqrt(QK_NOPE + QK_ROPE)),
        grid=(N // tm,),
        in_specs=[pl.BlockSpec((tm, D_MODEL), row),
                  _const_spec((1, D_MODEL)),
                  _const_spec(w_in_p.shape),
                  _const_spec((1, Q_LORA)),
                  _const_spec(wq_p.shape),
                  _const_spec((1, KV_LORA)),
                  _const_spec(wkv_p.shape),
                  pl.BlockSpec((tm, LANES), tab),
                  pl.BlockSpec((tm, LANES), tab)],
        out_specs=[pl.BlockSpec((tm, w), row) for w, _ in outs],
        out_shape=[jax.ShapeDtypeStruct((N, w), dt) for w, dt in outs],
        compiler_params=_params(("parallel",)),
        name="input_projection",
    )(x2d, norm1_g, w_in_p, gq, wq_p, gkv, wkv_p, cos, sin)


def _mla_kernel(q_ref, k_ref, v_ref, o_ref):
    q = q_ref[0]
    k = k_ref[0]
    s = lax.dot_general(q, k, (((1,), (1,)), ((), ())), preferred_element_type=jnp.float32)
    m = jnp.max(s, axis=-1, keepdims=True)
    p = jnp.exp(s - m)
    l = jnp.sum(p, axis=-1, keepdims=True)
    o = jnp.dot(p.astype(jnp.bfloat16), v_ref[0], preferred_element_type=jnp.float32)
    o_ref[0] = (o / l).astype(o_ref.dtype)


def _mla_attention(qa, ka, va):
    B, S, _ = qa.shape
    tq = TQ_MLA
    return pl.pallas_call(
        _mla_kernel,
        grid=(B, H_A, S // tq),
        in_specs=[pl.BlockSpec((1, tq, QK_PAD), lambda b, h, i: (b, i, h)),
                  pl.BlockSpec((1, S, QK_PAD), lambda b, h, i: (b, 0, h)),
                  pl.BlockSpec((1, S, V_DIM), lambda b, h, i: (b, 0, h))],
        out_specs=pl.BlockSpec((1, tq, V_DIM), lambda b, h, i: (b, i, h)),
        out_shape=jax.ShapeDtypeStruct((B, S, H_A * V_DIM), jnp.float32),
        compiler_params=_params(("parallel", "parallel", "arbitrary")),
        name="mla_attention",
    )(qa, ka, va)


def _window_kernel(sink_ref, q_ref, kp_ref, kc_ref, kn_ref, vp_ref, vc_ref, vn_ref, bias_ref, o_ref,
                   *, scale_b, seq_len):
    n = pl.program_id(1)
    q = q_ref[0]
    k = jnp.concatenate([kp_ref[0], kc_ref[0], kn_ref[0]], axis=0)
    v = jnp.concatenate([vp_ref[0], vc_ref[0], vn_ref[0]], axis=0)
    kt = k.astype(jnp.float32).T.astype(jnp.bfloat16)
    col = lax.broadcasted_iota(jnp.int32, (1, SPAN), 1)
    key_pos = n * Q_BLOCK - WINDOW + col
    in_seq = (key_pos >= 0) & (key_pos < seq_len)
    outs = []
    for hd in range(H_B):
        g = hd // GROUP
        s = jnp.dot(q[:, hd * HD_B:(hd + 1) * HD_B], kt[g * HD_B:(g + 1) * HD_B, :],
                    preferred_element_type=jnp.float32)
        s = jnp.where(in_seq, s * scale_b + bias_ref[hd], NEG_INF)
        sink = sink_ref[0, hd]
        m = jnp.maximum(jnp.max(s, axis=-1, keepdims=True), sink)
        p = jnp.exp(s - m)
        l = jnp.sum(p, axis=-1, keepdims=True) + jnp.exp(sink - m)
        o = jnp.dot(p.astype(jnp.bfloat16), v[:, g * HD_B:(g + 1) * HD_B],
                    preferred_element_type=jnp.float32)
        outs.append(o / l)
    o_ref[0] = jnp.concatenate(outs, axis=-1).astype(o_ref.dtype)


def _window_attention(qb, kb, vb, bias, sinks):
    B, S, _ = qb.shape
    nblk = S // Q_BLOCK
    kvw = KV_B * HD_B
    prev = lambda b, n: (b, jnp.maximum(n - 1, 0), 0)
    cur = lambda b, n: (b, n, 0)
    nxt = lambda b, n: (b, jnp.minimum(n + 1, nblk - 1), 0)
    kv_spec = lambda im: pl.BlockSpec((1, Q_BLOCK, kvw), im)
    return pl.pallas_call(
        functools.partial(_window_kernel, scale_b=1.0 / math.sqrt(HD_B), seq_len=S),
        grid=(B, nblk),
        in_specs=[pl.BlockSpec(memory_space=pltpu.SMEM),
                  pl.BlockSpec((1, Q_BLOCK, H_B * HD_B), cur),
                  kv_spec(prev), kv_spec(cur), kv_spec(nxt),
                  kv_spec(prev), kv_spec(cur), kv_spec(nxt),
                  _const_spec((H_B, Q_BLOCK, SPAN))],
        out_specs=pl.BlockSpec((1, Q_BLOCK, H_B * HD_B), cur),
        out_shape=jax.ShapeDtypeStruct((B, S, H_B * HD_B), jnp.float32),
        compiler_params=_params(("parallel", "arbitrary")),
        name="window_attention",
    )(sinks, qb, kb, kb, kb, vb, vb, vb, bias)


def _out_kernel(x_ref, oa_ref, ob_ref, gate_ref, w_ref, g2_ref, x1_ref, h2_ref):
    ga = gate_ref[:, 0:D_MODEL]
    gb = gate_ref[:, D_MODEL:2 * D_MODEL]
    mixed = jax.nn.sigmoid(ga) * oa_ref[...] + jax.nn.sigmoid(gb) * ob_ref[...]
    x1 = x_ref[...] + jnp.dot(mixed.astype(jnp.bfloat16), w_ref[...], preferred_element_type=jnp.float32)
    x1_ref[...] = x1
    h2_ref[...] = _rms(x1, g2_ref[...]).astype(jnp.bfloat16)


def _output_projection(x2d, oa, ob, gates, w_out, norm2_g):
    N = x2d.shape[0]
    tm = TM_OUT
    row = lambda i: (i, 0)
    return pl.pallas_call(
        _out_kernel,
        grid=(N // tm,),
        in_specs=[pl.BlockSpec((tm, D_MODEL), row),
                  pl.BlockSpec((tm, D_MODEL), row),
                  pl.BlockSpec((tm, D_MODEL), row),
                  pl.BlockSpec((tm, 2 * D_MODEL), row),
                  _const_spec(w_out.shape),
                  _const_spec((1, D_MODEL))],
        out_specs=[pl.BlockSpec((tm, D_MODEL), row), pl.BlockSpec((tm, D_MODEL), row)],
        out_shape=[jax.ShapeDtypeStruct((N, D_MODEL), jnp.float32),
                   jax.ShapeDtypeStruct((N, D_MODEL), jnp.bfloat16)],
        compiler_params=_params(("parallel",)),
        name="output_projection",
    )(x2d, oa, ob, gates, w_out, norm2_g)


def _ffn_kernel(x1_ref, hp_ref, hc_ref, hn_ref, wup_ref, cw_ref, cb_ref, wdn_ref, gf_ref, o_ref, acc_ref,
                *, n_tiles, final_norm):
    i = pl.program_id(1)
    tm = hc_ref.shape[1]
    hp = jnp.where(i > 0, hp_ref[0], jnp.zeros_like(hp_ref[0]))
    hn = jnp.where(i < n_tiles - 1, hn_ref[0], jnp.zeros_like(hn_ref[0]))
    he = jnp.concatenate([hp, hc_ref[0], hn], axis=0)
    acc_ref[...] = jnp.zeros_like(acc_ref)
    n_chunks = D_FF // FF_CHUNK

    def body(j, carry):
        c0 = pl.multiple_of(j * (2 * FF_CHUNK), 2 * FF_CHUNK)
        u = jnp.dot(he, wup_ref[:, pl.ds(c0, 2 * FF_CHUNK)], preferred_element_type=jnp.float32)
        cw = cw_ref[:, pl.ds(c0, 2 * FF_CHUNK)]
        cb = cb_ref[:, pl.ds(c0, 2 * FF_CHUNK)]
        y = (u[HALO - 1:HALO - 1 + tm] * cw[0:1] + u[HALO:HALO + tm] * cw[1:2]
             + u[HALO + 1:HALO + 1 + tm] * cw[2:3] + cb)
        gate = y[:, 0:FF_CHUNK]
        val = y[:, FF_CHUNK:2 * FF_CHUNK]
        act = (jax.nn.silu(gate) * val).astype(jnp.bfloat16)
        r0 = pl.multiple_of(j * FF_CHUNK, FF_CHUNK)
        acc_ref[...] += jnp.dot(act, wdn_ref[pl.ds(r0, FF_CHUNK), :], preferred_element_type=jnp.float32)
        return carry

    lax.fori_loop(0, n_chunks, body, 0)
    x2 = x1_ref[0] + acc_ref[...]
    o_ref[0] = _rms(x2, gf_ref[...]) if final_norm else x2


def _conv_ffn(x1, h2, wup_r, cw_r, cb_r, wdn, final_g, final_norm):
    B, S, _ = x1.shape
    tm = TM_FFN
    n_tiles = S // tm
    hb = tm // HALO
    cur = lambda b, i: (b, i, 0)
    prev = lambda b, i: (b, jnp.maximum(i * hb - 1, 0), 0)
    nxt = lambda b, i: (b, jnp.minimum((i + 1) * hb, S // HALO - 1), 0)
    return pl.pallas_call(
        functools.partial(_ffn_kernel, n_tiles=n_tiles, final_norm=final_norm),
        grid=(B, n_tiles),
        in_specs=[pl.BlockSpec((1, tm, D_MODEL), cur),
                  pl.BlockSpec((1, HALO, D_MODEL), prev),
                  pl.BlockSpec((1, tm, D_MODEL), cur),
                  pl.BlockSpec((1, HALO, D_MODEL), nxt),
                  _const_spec(wup_r.shape),
                  _const_spec(cw_r.shape),
                  _const_spec(cb_r.shape),
                  _const_spec(wdn.shape),
                  _const_spec((1, D_MODEL))],
        out_specs=pl.BlockSpec((1, tm, D_MODEL), cur),
        out_shape=jax.ShapeDtypeStruct((B, S, D_MODEL), jnp.float32),
        scratch_shapes=[pltpu.VMEM((tm, D_MODEL), jnp.float32)],
        compiler_params=_params(("parallel", "arbitrary")),
        name="conv_ffn",
    )(x1, h2, h2, h2, wup_r, cw_r, cb_r, wdn, final_g)


def _prep_w_in(w):
    pad = jnp.zeros((D_MODEL, LANES - QK_ROPE), w.dtype)
    split = Q_LORA + KV_LORA + QK_ROPE
    return jnp.concatenate([w[:, :split], pad, w[:, split:]], axis=1).astype(jnp.bfloat16)


def _prep_w_q_b(w):
    w3 = w.reshape(Q_LORA, H_A, QK_NOPE + QK_ROPE)
    pad = jnp.zeros((Q_LORA, H_A, QK_PAD - QK_NOPE - QK_ROPE), w.dtype)
    return jnp.concatenate([w3, pad], axis=-1).reshape(Q_LORA, H_A * QK_PAD).astype(jnp.bfloat16)


def _prep_w_kv_b(w):
    w3 = w.reshape(KV_LORA, H_A, QK_NOPE + V_DIM)
    k = w3[:, :, :QK_NOPE].reshape(KV_LORA, H_A * QK_NOPE)
    v = w3[:, :, QK_NOPE:].reshape(KV_LORA, H_A * V_DIM)
    return jnp.concatenate([k, v], axis=1).astype(jnp.bfloat16)


def _interleave_ff(a):
    lead = a.shape[:-1]
    n = D_FF // FF_CHUNK
    a = a.reshape(*lead, 2, n, FF_CHUNK)
    a = jnp.swapaxes(a, -3, -2)
    return a.reshape(*lead, 2 * D_FF)


def kernel(x, positions, norm1_g, w_in, q_a_norm_g, w_q_b, kv_a_norm_g, w_kv_b, rel_bias, sinks,
           w_out, norm2_g, w_up, conv_w, conv_b, w_down, final_norm_g):
    B, S, D = x.shape
    depth = norm1_g.shape[0]
    cos, sin = _rope_tables(positions)
    bias = _window_bias(rel_bias)
    f32 = jnp.float32
    for l in range(depth):
        x2d = x.reshape(B * S, D)
        qa, ka, va, qb, kb, vb, gates = _input_projection(
            x2d, norm1_g[l][None].astype(f32), _prep_w_in(w_in[l]),
            q_a_norm_g[l][None].astype(f32), _prep_w_q_b(w_q_b[l]),
            kv_a_norm_g[l][None].astype(f32), _prep_w_kv_b(w_kv_b[l]), cos, sin, S)
        r3 = lambda a: a.reshape(B, S, a.shape[-1])
        oa = _mla_attention(r3(qa), r3(ka), r3(va))
        ob = _window_attention(r3(qb), r3(kb), r3(vb), bias, sinks[l][None].astype(f32))
        x1, h2 = _output_projection(x2d, oa.reshape(B * S, D), ob.reshape(B * S, D), gates,
                                    w_out[l].astype(jnp.bfloat16), norm2_g[l][None].astype(f32))
        x = _conv_ffn(x1.reshape(B, S, D), h2.reshape(B, S, D),
                      _interleave_ff(w_up[l]).astype(jnp.bfloat16),
                      _interleave_ff(conv_w[l]).astype(f32),
                      _interleave_ff(conv_b[l])[None].astype(f32),
                      w_down[l].astype(jnp.bfloat16), final_norm_g[None].astype(f32),
                      final_norm=(l == depth - 1))
    return x
```

```python
import functools
import math

import jax
import jax.numpy as jnp
import numpy as np
from jax import lax
from jax.experimental import pallas as pl
from jax.experimental.pallas import tpu as pltpu

D_MODEL = 1024
EPS = 1e-6
Q_BLOCK = 128
H_A = 8
QK_NOPE = 128
QK_ROPE = 64
V_DIM = 128
Q_LORA = 256
KV_LORA = 128
ROPE_THETA = 10000.0
H_B = 16
KV_B = 4
GROUP = H_B // KV_B
HD_B = 64
WINDOW = 128
NUM_BUCKETS = 32
MAX_DISTANCE = 128
D_FF = 2816
SPAN = Q_BLOCK + 2 * WINDOW

LANES = 128
QK_PAD = 256
NEG_INF = -1e30
LOG2E = math.log2(math.e)
VMEM_LIMIT = 56 * 1024 * 1024

TM_PROJ = 512
TQ_MLA = 256
NQ_WIN = 4
TM_OUT = 512
TM_FFN = 512
FF_CHUNK = 256
HALO = 16


def _const_spec(shape):
    nd = len(shape)
    return pl.BlockSpec(shape, lambda *_: (0,) * nd, pipeline_mode=pl.Buffered(1))


def _params(sem):
    return pltpu.CompilerParams(dimension_semantics=sem, vmem_limit_bytes=VMEM_LIMIT)


def _rms(x, g):
    return x * lax.rsqrt(jnp.mean(x * x, axis=-1, keepdims=True) + EPS) * g


def _swap_halves_32(x):
    lane = lax.broadcasted_iota(jnp.int32, x.shape, 1)
    up = pltpu.roll(x, LANES - 32, 1)
    down = pltpu.roll(x, 32, 1)
    return jnp.where((lane & 32) == 0, up, down)


def _rope_table_kernel(pos_ref, inv_ref, sign_ref, cos_ref, sin_ref):
    ang = pos_ref[...] * inv_ref[...]
    cos_ref[...] = jnp.cos(ang)
    sin_ref[...] = jnp.sin(ang) * sign_ref[...]


def _rope_tables(positions):
    S = positions.shape[0]
    half = QK_ROPE // 2
    inv_freq = ROPE_THETA ** (-jnp.arange(half, dtype=jnp.float32) / half)
    inv = jnp.tile(inv_freq, LANES // half)[None, :]
    sign = jnp.tile(jnp.concatenate([-jnp.ones(half, jnp.float32), jnp.ones(half, jnp.float32)]),
                    LANES // QK_ROPE)[None, :]
    pos = positions.astype(jnp.float32)[:, None]
    ts = 512
    return pl.pallas_call(
        _rope_table_kernel,
        grid=(S // ts,),
        in_specs=[pl.BlockSpec((ts, 1), lambda i: (i, 0)),
                  pl.BlockSpec((1, LANES), lambda i: (0, 0)),
                  pl.BlockSpec((1, LANES), lambda i: (0, 0))],
        out_specs=[pl.BlockSpec((ts, LANES), lambda i: (i, 0)),
                   pl.BlockSpec((ts, LANES), lambda i: (i, 0))],
        out_shape=[jax.ShapeDtypeStruct((S, LANES), jnp.float32)] * 2,
        compiler_params=_params(("parallel",)),
        name="rope_tables",
    )(pos, inv, sign)


def _t5_bucket(rel):
    nb = NUM_BUCKETS // 2
    max_exact = nb // 2
    base = (rel > 0).astype(jnp.int32) * nb
    n = jnp.abs(rel)
    nf = jnp.maximum(n, 1).astype(jnp.float32)
    large = max_exact + (jnp.log(nf / max_exact) / math.log(MAX_DISTANCE / max_exact)
                         * (nb - max_exact)).astype(jnp.int32)
    large = jnp.minimum(large, nb - 1)
    return base + jnp.where(n < max_exact, n, large)


def _bias_kernel(rel_bias_ref, bucket_ref, band_ref, out_ref):
    g = pl.program_id(0)
    bucket = bucket_ref[0]
    band = band_ref[0]
    for j in range(GROUP):
        acc = jnp.zeros(bucket.shape, jnp.float32)
        for b in range(NUM_BUCKETS):
            acc = jnp.where(bucket == b, rel_bias_ref[b, g * GROUP + j], acc)
        out_ref[0, 0, :, j * Q_BLOCK:(j + 1) * Q_BLOCK] = jnp.where(band != 0, acc * LOG2E, NEG_INF)


def _window_bias(rel_bias):
    a = jnp.arange(Q_BLOCK, dtype=jnp.int32)[None, :]
    c = jnp.arange(SPAN, dtype=jnp.int32)[:, None]
    rel = c - WINDOW - a
    bucket = _t5_bucket(rel).reshape(3, Q_BLOCK, Q_BLOCK)
    band = (jnp.abs(rel) <= WINDOW).astype(jnp.int32).reshape(3, Q_BLOCK, Q_BLOCK)
    part = pl.BlockSpec((1, Q_BLOCK, Q_BLOCK), lambda g, p: (p, 0, 0))
    return pl.pallas_call(
        _bias_kernel,
        grid=(KV_B, 3),
        in_specs=[pl.BlockSpec(memory_space=pltpu.SMEM), part, part],
        out_specs=pl.BlockSpec((1, 1, Q_BLOCK, GROUP * Q_BLOCK), lambda g, p: (g, p, 0, 0)),
        out_shape=jax.ShapeDtypeStruct((KV_B, 3, Q_BLOCK, GROUP * Q_BLOCK), jnp.float32),
        compiler_params=_params(("arbitrary", "arbitrary")),
        name="window_bias",
    )(rel_bias.astype(jnp.float32), bucket, band)


def _proj_kernel(x_ref, g1_ref, w_in_ref, gq_ref, wq_ref, gkv_ref, wkv_ref, cos_ref, sin_ref,
                 qa_ref, ka_ref, va_ref, qb_ref, kb_ref, vbt_ref, gate_ref, *, scale_a, scale_b):
    x = x_ref[...]
    h = _rms(x, g1_ref[...]).astype(jnp.bfloat16)
    cos = cos_ref[...]
    sin = sin_ref[...]

    def rope(blk):
        return blk * cos + _swap_halves_32(blk) * sin

    lat = jnp.dot(h, w_in_ref[:, 0:512], preferred_element_type=jnp.float32)
    q_lat = lat[:, 0:Q_LORA]
    c_kv = lat[:, Q_LORA:Q_LORA + KV_LORA]
    k_rope = rope(lat[:, 384:512]).astype(jnp.bfloat16)

    qn = _rms(q_lat, gq_ref[...]).astype(jnp.bfloat16)
    q = jnp.dot(qn, wq_ref[...], preferred_element_type=jnp.float32)
    for hd in range(H_A):
        base = hd * QK_PAD
        qa_ref[:, base:base + QK_NOPE] = (q[:, base:base + QK_NOPE] * scale_a).astype(jnp.bfloat16)
        qa_ref[:, base + QK_NOPE:base + QK_PAD] = (
            rope(q[:, base + QK_NOPE:base + QK_PAD]) * scale_a).astype(jnp.bfloat16)

    cn = _rms(c_kv, gkv_ref[...]).astype(jnp.bfloat16)
    kv = jnp.dot(cn, wkv_ref[...], preferred_element_type=jnp.float32)
    for hd in range(H_A):
        base = hd * QK_PAD
        ka_ref[:, base:base + QK_NOPE] = kv[:, hd * QK_NOPE:(hd + 1) * QK_NOPE].astype(jnp.bfloat16)
        ka_ref[:, base + QK_NOPE:base + QK_PAD] = k_rope
    va_ref[...] = kv[:, H_A * QK_NOPE:].astype(jnp.bfloat16)

    qkv_b = jnp.dot(h, w_in_ref[:, 512:2048], preferred_element_type=jnp.float32)
    qb_ref[...] = (qkv_b[:, 0:1024] * scale_b).astype(jnp.bfloat16)
    lane = lax.broadcasted_iota(jnp.int32, (x.shape[0], LANES), 1)
    for g in range(KV_B):
        pair = qkv_b[:, 1024 + (g // 2) * LANES:1024 + (g // 2 + 1) * LANES]
        other = pltpu.roll(pair, HD_B, 1)
        first = (lane < HD_B) if g % 2 == 0 else (lane >= HD_B)
        rep = jnp.where(first, pair, other).astype(jnp.bfloat16)
        kb_ref[:, g * 2 * LANES:(g * 2 + 1) * LANES] = rep
        kb_ref[:, (g * 2 + 1) * LANES:(g * 2 + 2) * LANES] = rep
    vbt_ref[...] = qkv_b[:, 1280:1536].T.astype(jnp.bfloat16)

    gate_ref[...] = jnp.dot(h, w_in_ref[:, 2048:4096], preferred_element_type=jnp.float32)


def _input_projection(x2d, norm1_g, w_in_p, gq, wq_p, gkv, wkv_p, cos, sin, S):
    N = x2d.shape[0]
    tm = TM_PROJ
    s_tiles = S // tm
    row = lambda i: (i, 0)
    tab = lambda i: (i % s_tiles, 0)
    bf = jnp.bfloat16
    col = lambda i: (0, i)
    kvw = KV_B * HD_B
    outs = [((N, H_A * QK_PAD), (tm, H_A * QK_PAD), row, bf),
            ((N, H_A * QK_PAD), (tm, H_A * QK_PAD), row, bf),
            ((N, H_A * V_DIM), (tm, H_A * V_DIM), row, bf),
            ((N, H_B * HD_B), (tm, H_B * HD_B), row, bf),
            ((N, GROUP * kvw), (tm, GROUP * kvw), row, bf),
            ((kvw, N), (kvw, tm), col, bf),
            ((N, 2 * D_MODEL), (tm, 2 * D_MODEL), row, jnp.float32)]
    return pl.pallas_call(
        functools.partial(_proj_kernel, scale_a=1.0 / math.sqrt(QK_NOPE + QK_ROPE),
                          scale_b=LOG2E / math.sqrt(HD_B)),
        grid=(N // tm,),
        in_specs=[pl.BlockSpec((tm, D_MODEL), row),
                  _const_spec((1, D_MODEL)),
                  _const_spec(w_in_p.shape),
                  _const_spec((1, Q_LORA)),
                  _const_spec(wq_p.shape),
                  _const_spec((1, KV_LORA)),
                  _const_spec(wkv_p.shape),
                  pl.BlockSpec((tm, LANES), tab),
                  pl.BlockSpec((tm, LANES), tab)],
        out_specs=[pl.BlockSpec(blk, im) for _, blk, im, _ in outs],
        out_shape=[jax.ShapeDtypeStruct(shp, dt) for shp, _, _, dt in outs],
        compiler_params=_params(("parallel",)),
        name="input_projection",
    )(x2d, norm1_g, w_in_p, gq, wq_p, gkv, wkv_p, cos, sin)


def _mla_kernel(q_ref, k_ref, v_ref, o_ref):
    q = q_ref[0]
    k = k_ref[0]
    s = lax.dot_general(q, k, (((1,), (1,)), ((), ())), preferred_element_type=jnp.float32)
    m = jnp.max(s, axis=-1, keepdims=True)
    p = jnp.exp(s - m)
    l = jnp.sum(p, axis=-1, keepdims=True)
    o = jnp.dot(p.astype(jnp.bfloat16), v_ref[0], preferred_element_type=jnp.float32)
    o_ref[0] = (o / l).astype(o_ref.dtype)


def _mla_attention(qa, ka, va):
    B, S, _ = qa.shape
    tq = TQ_MLA
    return pl.pallas_call(
        _mla_kernel,
        grid=(B, H_A, S // tq),
        in_specs=[pl.BlockSpec((1, tq, QK_PAD), lambda b, h, i: (b, i, h)),
                  pl.BlockSpec((1, S, QK_PAD), lambda b, h, i: (b, 0, h)),
                  pl.BlockSpec((1, S, V_DIM), lambda b, h, i: (b, 0, h))],
        out_specs=pl.BlockSpec((1, tq, V_DIM), lambda b, h, i: (b, i, h)),
        out_shape=jax.ShapeDtypeStruct((B, S, H_A * V_DIM), jnp.float32),
        compiler_params=_params(("parallel", "parallel", "arbitrary")),
        name="mla_attention",
    )(qa, ka, va)


def _window_kernel(q_ref, kp_ref, kc_ref, kn_ref, vp_ref, vc_ref, vn_ref, bias_ref, sink_ref, o_ref, ot_ref,
                   *, n_steps):
    step = pl.program_id(1)
    gw = GROUP * HD_B
    k_all = jnp.concatenate([kp_ref[0], kc_ref[0], kn_ref[0]], axis=0)
    vt_all = jnp.concatenate([vp_ref[...], vc_ref[...], vn_ref[...]], axis=1)
    lane = lax.broadcasted_iota(jnp.int32, (Q_BLOCK, gw), 1)
    for i in range(NQ_WIN):
        q_blk = q_ref[0, i * Q_BLOCK:(i + 1) * Q_BLOCK, :]
        for g in range(KV_B):
            qg = q_blk[:, g * gw:(g + 1) * gw]
            wt = jnp.concatenate(
                [jnp.where((lane >= j * HD_B) & (lane < (j + 1) * HD_B), qg, jnp.zeros_like(qg))
                 for j in range(GROUP)], axis=0)
            kg = k_all[i * Q_BLOCK:i * Q_BLOCK + SPAN, g * gw:(g + 1) * gw]
            s = lax.dot_general(kg, wt, (((1,), (1,)), ((), ())),
                                preferred_element_type=jnp.float32)
            parts = []
            for part in range(3):
                sp = s[part * Q_BLOCK:(part + 1) * Q_BLOCK] + bias_ref[g, part]
                if part == 0 and i == 0:
                    sp = jnp.where(step > 0, sp, NEG_INF)
                if part == 2 and i == NQ_WIN - 1:
                    sp = jnp.where(step < n_steps - 1, sp, NEG_INF)
                parts.append(sp)
            s = jnp.concatenate(parts, axis=0)
            sink = sink_ref[g]
            m = jnp.maximum(jnp.max(s, axis=0, keepdims=True), sink)
            p = jnp.exp2(s - m)
            l = jnp.sum(p, axis=0, keepdims=True) + jnp.exp2(sink - m)
            vt = vt_all[g * HD_B:(g + 1) * HD_B, i * Q_BLOCK:i * Q_BLOCK + SPAN]
            r = jnp.dot(vt, p.astype(jnp.bfloat16), preferred_element_type=jnp.float32) / l
            for j in range(GROUP):
                hd = g * GROUP + j
                ot_ref[hd * HD_B:(hd + 1) * HD_B, i * Q_BLOCK:(i + 1) * Q_BLOCK] = (
                    r[:, j * Q_BLOCK:(j + 1) * Q_BLOCK])
    o_ref[0] = ot_ref[...].T.astype(o_ref.dtype)


def _window_attention(qb, kb_rep, vbt, bias, sink_rows):
    B, S, _ = qb.shape
    nblk = S // Q_BLOCK
    rows = NQ_WIN * Q_BLOCK
    n_steps = S // rows
    w = H_B * HD_B
    kprev = lambda b, s: (b, jnp.maximum(s * NQ_WIN - 1, 0), 0)
    cur = lambda b, s: (b, s, 0)
    knext = lambda b, s: (b, jnp.minimum((s + 1) * NQ_WIN, nblk - 1), 0)
    vprev = lambda b, s: (0, b * nblk + jnp.maximum(s * NQ_WIN - 1, 0))
    vcur = lambda b, s: (0, b * n_steps + s)
    vnext = lambda b, s: (0, b * nblk + jnp.minimum((s + 1) * NQ_WIN, nblk - 1))
    kvw = KV_B * HD_B
    return pl.pallas_call(
        functools.partial(_window_kernel, n_steps=n_steps),
        grid=(B, n_steps),
        in_specs=[pl.BlockSpec((1, rows, w), cur),
                  pl.BlockSpec((1, Q_BLOCK, w), kprev),
                  pl.BlockSpec((1, rows, w), cur),
                  pl.BlockSpec((1, Q_BLOCK, w), knext),
                  pl.BlockSpec((kvw, Q_BLOCK), vprev),
                  pl.BlockSpec((kvw, rows), vcur),
                  pl.BlockSpec((kvw, Q_BLOCK), vnext),
                  _const_spec(bias.shape),
                  _const_spec(sink_rows.shape)],
        out_specs=pl.BlockSpec((1, rows, w), cur),
        out_shape=jax.ShapeDtypeStruct((B, S, w), jnp.float32),
        scratch_shapes=[pltpu.VMEM((w, rows), jnp.float32)],
        compiler_params=_params(("parallel", "arbitrary")),
        name="window_attention",
    )(qb, kb_rep, kb_rep, kb_rep, vbt, vbt, vbt, bias, sink_rows)


def _out_kernel(x_ref, oa_ref, ob_ref, gate_ref, w_ref, g2_ref, x1_ref, h2_ref):
    ga = gate_ref[:, 0:D_MODEL]
    gb = gate_ref[:, D_MODEL:2 * D_MODEL]
    mixed = jax.nn.sigmoid(ga) * oa_ref[...] + jax.nn.sigmoid(gb) * ob_ref[...]
    x1 = x_ref[...] + jnp.dot(mixed.astype(jnp.bfloat16), w_ref[...], preferred_element_type=jnp.float32)
    x1_ref[...] = x1
    h2_ref[...] = _rms(x1, g2_ref[...]).astype(jnp.bfloat16)


def _output_projection(x2d, oa, ob, gates, w_out, norm2_g):
    N = x2d.shape[0]
    tm = TM_OUT
    row = lambda i: (i, 0)
    return pl.pallas_call(
        _out_kernel,
        grid=(N // tm,),
        in_specs=[pl.BlockSpec((tm, D_MODEL), row),
                  pl.BlockSpec((tm, D_MODEL), row),
                  pl.BlockSpec((tm, D_MODEL), row),
                  pl.BlockSpec((tm, 2 * D_MODEL), row),
                  _const_spec(w_out.shape),
                  _const_spec((1, D_MODEL))],
        out_specs=[pl.BlockSpec((tm, D_MODEL), row), pl.BlockSpec((tm, D_MODEL), row)],
        out_shape=[jax.ShapeDtypeStruct((N, D_MODEL), jnp.float32),
                   jax.ShapeDtypeStruct((N, D_MODEL), jnp.bfloat16)],
        compiler_params=_params(("parallel",)),
        name="output_projection",
    )(x2d, oa, ob, gates, w_out, norm2_g)


def _ffn_kernel(x1_ref, hp_ref, hc_ref, hn_ref, wup_ref, cw_ref, cb_ref, wdn_ref, gf_ref, o_ref, acc_ref,
                *, n_tiles, final_norm):
    i = pl.program_id(1)
    tm = hc_ref.shape[1]
    hp = jnp.where(i > 0, hp_ref[0], jnp.zeros_like(hp_ref[0]))
    hn = jnp.where(i < n_tiles - 1, hn_ref[0], jnp.zeros_like(hn_ref[0]))
    he = jnp.concatenate([hp, hc_ref[0], hn], axis=0)
    acc_ref[...] = jnp.zeros_like(acc_ref)
    n_chunks = D_FF // FF_CHUNK

    def body(j, carry):
        c0 = pl.multiple_of(j * (2 * FF_CHUNK), 2 * FF_CHUNK)
        u = jnp.dot(he, wup_ref[:, pl.ds(c0, 2 * FF_CHUNK)], preferred_element_type=jnp.float32)
        cw = cw_ref[:, pl.ds(c0, 2 * FF_CHUNK)]
        cb = cb_ref[:, pl.ds(c0, 2 * FF_CHUNK)]
        y = (u[HALO - 1:HALO - 1 + tm] * cw[0:1] + u[HALO:HALO + tm] * cw[1:2]
             + u[HALO + 1:HALO + 1 + tm] * cw[2:3] + cb)
        gate = y[:, 0:FF_CHUNK]
        val = y[:, FF_CHUNK:2 * FF_CHUNK]
        act = (jax.nn.silu(gate) * val).astype(jnp.bfloat16)
        r0 = pl.multiple_of(j * FF_CHUNK, FF_CHUNK)
        acc_ref[...] += jnp.dot(act, wdn_ref[pl.ds(r0, FF_CHUNK), :], preferred_element_type=jnp.float32)
        return carry

    lax.fori_loop(0, n_chunks, body, 0)
    x2 = x1_ref[0] + acc_ref[...]
    o_ref[0] = _rms(x2, gf_ref[...]) if final_norm else x2


def _conv_ffn(x1, h2, wup_r, cw_r, cb_r, wdn, final_g, final_norm):
    B, S, _ = x1.shape
    tm = TM_FFN
    n_tiles = S // tm
    hb = tm // HALO
    cur = lambda b, i: (b, i, 0)
    prev = lambda b, i: (b, jnp.maximum(i * hb - 1, 0), 0)
    nxt = lambda b, i: (b, jnp.minimum((i + 1) * hb, S // HALO - 1), 0)
    return pl.pallas_call(
        functools.partial(_ffn_kernel, n_tiles=n_tiles, final_norm=final_norm),
        grid=(B, n_tiles),
        in_specs=[pl.BlockSpec((1, tm, D_MODEL), cur),
                  pl.BlockSpec((1, HALO, D_MODEL), prev),
                  pl.BlockSpec((1, tm, D_MODEL), cur),
                  pl.BlockSpec((1, HALO, D_MODEL), nxt),
                  _const_spec(wup_r.shape),
                  _const_spec(cw_r.shape),
                  _const_spec(cb_r.shape),
                  _const_spec(wdn.shape),
                  _const_spec((1, D_MODEL))],
        out_specs=pl.BlockSpec((1, tm, D_MODEL), cur),
        out_shape=jax.ShapeDtypeStruct((B, S, D_MODEL), jnp.float32),
        scratch_shapes=[pltpu.VMEM((tm, D_MODEL), jnp.float32)],
        compiler_params=_params(("parallel", "arbitrary")),
        name="conv_ffn",
    )(x1, h2, h2, h2, wup_r, cw_r, cb_r, wdn, final_g)


def _prep_w_in(w):
    pad = jnp.zeros((D_MODEL, LANES - QK_ROPE), w.dtype)
    split = Q_LORA + KV_LORA + QK_ROPE
    return jnp.concatenate([w[:, :split], pad, w[:, split:]], axis=1).astype(jnp.bfloat16)


def _prep_w_q_b(w):
    w3 = w.reshape(Q_LORA, H_A, QK_NOPE + QK_ROPE)
    pad = jnp.zeros((Q_LORA, H_A, QK_PAD - QK_NOPE - QK_ROPE), w.dtype)
    return jnp.concatenate([w3, pad], axis=-1).reshape(Q_LORA, H_A * QK_PAD).astype(jnp.bfloat16)


def _prep_w_kv_b(w):
    w3 = w.reshape(KV_LORA, H_A, QK_NOPE + V_DIM)
    k = w3[:, :, :QK_NOPE].reshape(KV_LORA, H_A * QK_NOPE)
    v = w3[:, :, QK_NOPE:].reshape(KV_LORA, H_A * V_DIM)
    return jnp.concatenate([k, v], axis=1).astype(jnp.bfloat16)


def _interleave_ff(a):
    lead = a.shape[:-1]
    n = D_FF // FF_CHUNK
    a = a.reshape(*lead, 2, n, FF_CHUNK)
    a = jnp.swapaxes(a, -3, -2)
    return a.reshape(*lead, 2 * D_FF)


def kernel(x, positions, norm1_g, w_in, q_a_norm_g, w_q_b, kv_a_norm_g, w_kv_b, rel_bias, sinks,
           w_out, norm2_g, w_up, conv_w, conv_b, w_down, final_norm_g):
    B, S, D = x.shape
    depth = norm1_g.shape[0]
    cos, sin = _rope_tables(positions)
    bias = _window_bias(rel_bias)
    f32 = jnp.float32
    for l in range(depth):
        x2d = x.reshape(B * S, D)
        qa, ka, va, qb, kb_rep, vbt, gates = _input_projection(
            x2d, norm1_g[l][None].astype(f32), _prep_w_in(w_in[l]),
            q_a_norm_g[l][None].astype(f32), _prep_w_q_b(w_q_b[l]),
            kv_a_norm_g[l][None].astype(f32), _prep_w_kv_b(w_kv_b[l]), cos, sin, S)
        r3 = lambda a: a.reshape(B, S, a.shape[-1])
        oa = _mla_attention(r3(qa), r3(ka), r3(va))
        sink_rows = jnp.repeat(sinks[l].astype(f32) * LOG2E, Q_BLOCK).reshape(KV_B, 1, GROUP * Q_BLOCK)
        ob = _window_attention(r3(qb), r3(kb_rep), vbt, bias, sink_rows)
        x1, h2 = _output_projection(x2d, oa.reshape(B * S, D), ob.reshape(B * S, D), gates,
                                    w_out[l].astype(jnp.bfloat16), norm2_g[l][None].astype(f32))
        x = _conv_ffn(x1.reshape(B, S, D), h2.reshape(B, S, D),
                      _interleave_ff(w_up[l]).astype(jnp.bfloat16),
                      _interleave_ff(conv_w[l]).astype(f32),
                      _interleave_ff(conv_b[l])[None].astype(f32),
                      w_down[l].astype(jnp.bfloat16), final_norm_g[None].astype(f32),
                      final_norm=(l == depth - 1))
    return x
```

```python
import functools
import math

import jax
import jax.numpy as jnp
import numpy as np
from jax import lax
from jax.experimental import pallas as pl
from jax.experimental.pallas import tpu as pltpu

D_MODEL = 1024
EPS = 1e-6
Q_BLOCK = 128
H_A = 8
QK_NOPE = 128
QK_ROPE = 64
V_DIM = 128
Q_LORA = 256
KV_LORA = 128
ROPE_THETA = 10000.0
H_B = 16
KV_B = 4
GROUP = H_B // KV_B
HD_B = 64
WINDOW = 128
NUM_BUCKETS = 32
MAX_DISTANCE = 128
D_FF = 2816
SPAN = Q_BLOCK + 2 * WINDOW

LANES = 128
QK_PAD = 256
NEG_INF = -1e30
LOG2E = math.log2(math.e)
VMEM_LIMIT = 56 * 1024 * 1024

TM_PROJ = 512
TQ_MLA = 512
TK_MLA = 1024
MLA_LOOKAHEAD = 1
NQ_WIN = 4
TM_OUT = 512
TM_FFN = 512
FF_CHUNK = 256
HALO = 16


def _const_spec(shape):
    nd = len(shape)
    return pl.BlockSpec(shape, lambda *_: (0,) * nd, pipeline_mode=pl.Buffered(1))


def _params(sem):
    return pltpu.CompilerParams(dimension_semantics=sem, vmem_limit_bytes=VMEM_LIMIT)


def _rms(x, g):
    return x * lax.rsqrt(jnp.mean(x * x, axis=-1, keepdims=True) + EPS) * g


def _swap_halves_32(x):
    lane = lax.broadcasted_iota(jnp.int32, x.shape, 1)
    up = pltpu.roll(x, LANES - 32, 1)
    down = pltpu.roll(x, 32, 1)
    return jnp.where((lane & 32) == 0, up, down)


def _rope_table_kernel(pos_ref, inv_ref, sign_ref, cos_ref, sin_ref):
    ang = pos_ref[...] * inv_ref[...]
    cos_ref[...] = jnp.cos(ang)
    sin_ref[...] = jnp.sin(ang) * sign_ref[...]


def _rope_tables(positions):
    S = positions.shape[0]
    half = QK_ROPE // 2
    inv_freq = ROPE_THETA ** (-jnp.arange(half, dtype=jnp.float32) / half)
    inv = jnp.tile(inv_freq, LANES // half)[None, :]
    sign = jnp.tile(jnp.concatenate([-jnp.ones(half, jnp.float32), jnp.ones(half, jnp.float32)]),
                    LANES // QK_ROPE)[None, :]
    pos = positions.astype(jnp.float32)[:, None]
    ts = 512
    return pl.pallas_call(
        _rope_table_kernel,
        grid=(S // ts,),
        in_specs=[pl.BlockSpec((ts, 1), lambda i: (i, 0)),
                  pl.BlockSpec((1, LANES), lambda i: (0, 0)),
                  pl.BlockSpec((1, LANES), lambda i: (0, 0))],
        out_specs=[pl.BlockSpec((ts, LANES), lambda i: (i, 0)),
                   pl.BlockSpec((ts, LANES), lambda i: (i, 0))],
        out_shape=[jax.ShapeDtypeStruct((S, LANES), jnp.float32)] * 2,
        compiler_params=_params(("parallel",)),
        name="rope_tables",
    )(pos, inv, sign)


def _t5_bucket(rel):
    nb = NUM_BUCKETS // 2
    max_exact = nb // 2
    base = (rel > 0).astype(jnp.int32) * nb
    n = jnp.abs(rel)
    nf = jnp.maximum(n, 1).astype(jnp.float32)
    large = max_exact + (jnp.log(nf / max_exact) / math.log(MAX_DISTANCE / max_exact)
                         * (nb - max_exact)).astype(jnp.int32)
    large = jnp.minimum(large, nb - 1)
    return base + jnp.where(n < max_exact, n, large)


def _bias_kernel(rel_bias_ref, bucket_ref, band_ref, out_ref):
    g = pl.program_id(0)
    bucket = bucket_ref[0]
    band = band_ref[0]
    for j in range(GROUP):
        acc = jnp.zeros(bucket.shape, jnp.float32)
        for b in range(NUM_BUCKETS):
            acc = jnp.where(bucket == b, rel_bias_ref[b, g * GROUP + j], acc)
        out_ref[0, 0, :, j * Q_BLOCK:(j + 1) * Q_BLOCK] = jnp.where(band != 0, acc * LOG2E, NEG_INF)


def _window_bias(rel_bias):
    a = jnp.arange(Q_BLOCK, dtype=jnp.int32)[None, :]
    c = jnp.arange(SPAN, dtype=jnp.int32)[:, None]
    rel = c - WINDOW - a
    bucket = _t5_bucket(rel).reshape(3, Q_BLOCK, Q_BLOCK)
    band = (jnp.abs(rel) <= WINDOW).astype(jnp.int32).reshape(3, Q_BLOCK, Q_BLOCK)
    part = pl.BlockSpec((1, Q_BLOCK, Q_BLOCK), lambda g, p: (p, 0, 0))
    return pl.pallas_call(
        _bias_kernel,
        grid=(KV_B, 3),
        in_specs=[pl.BlockSpec(memory_space=pltpu.SMEM), part, part],
        out_specs=pl.BlockSpec((1, 1, Q_BLOCK, GROUP * Q_BLOCK), lambda g, p: (g, p, 0, 0)),
        out_shape=jax.ShapeDtypeStruct((KV_B, 3, Q_BLOCK, GROUP * Q_BLOCK), jnp.float32),
        compiler_params=_params(("arbitrary", "arbitrary")),
        name="window_bias",
    )(rel_bias.astype(jnp.float32), bucket, band)


def _proj_kernel(x_ref, g1_ref, w_in_ref, gq_ref, wq_ref, gkv_ref, wkv_ref, cos_ref, sin_ref,
                 qa_ref, ka_ref, vat_ref, qb_ref, kb_ref, vbt_ref, gate_ref, *, scale_a, scale_b):
    x = x_ref[...]
    h = _rms(x, g1_ref[...]).astype(jnp.bfloat16)
    cos = cos_ref[...]
    sin = sin_ref[...]

    def rope(blk):
        return blk * cos + _swap_halves_32(blk) * sin

    lat = jnp.dot(h, w_in_ref[:, 0:512], preferred_element_type=jnp.float32)
    q_lat = lat[:, 0:Q_LORA]
    c_kv = lat[:, Q_LORA:Q_LORA + KV_LORA]
    k_rope = rope(lat[:, 384:512]).astype(jnp.bfloat16)

    qn = _rms(q_lat, gq_ref[...]).astype(jnp.bfloat16)
    q = jnp.dot(qn, wq_ref[...], preferred_element_type=jnp.float32)
    for hd in range(H_A):
        base = hd * QK_PAD
        qa_ref[:, base:base + QK_NOPE] = (q[:, base:base + QK_NOPE] * scale_a).astype(jnp.bfloat16)
        qa_ref[:, base + QK_NOPE:base + QK_PAD] = (
            rope(q[:, base + QK_NOPE:base + QK_PAD]) * scale_a).astype(jnp.bfloat16)

    cn = _rms(c_kv, gkv_ref[...]).astype(jnp.bfloat16)
    kv = jnp.dot(cn, wkv_ref[...], preferred_element_type=jnp.float32)
    for hd in range(H_A):
        base = hd * QK_PAD
        ka_ref[:, base:base + QK_NOPE] = kv[:, hd * QK_NOPE:(hd + 1) * QK_NOPE].astype(jnp.bfloat16)
        ka_ref[:, base + QK_NOPE:base + QK_PAD] = k_rope
    vat_ref[...] = kv[:, H_A * QK_NOPE:].T.astype(jnp.bfloat16)

    qkv_b = jnp.dot(h, w_in_ref[:, 512:2048], preferred_element_type=jnp.float32)
    qb_ref[...] = (qkv_b[:, 0:1024] * scale_b).astype(jnp.bfloat16)
    lane = lax.broadcasted_iota(jnp.int32, (x.shape[0], LANES), 1)
    for g in range(KV_B):
        pair = qkv_b[:, 1024 + (g // 2) * LANES:1024 + (g // 2 + 1) * LANES]
        other = pltpu.roll(pair, HD_B, 1)
        first = (lane < HD_B) if g % 2 == 0 else (lane >= HD_B)
        rep = jnp.where(first, pair, other).astype(jnp.bfloat16)
        kb_ref[:, g * 2 * LANES:(g * 2 + 1) * LANES] = rep
        kb_ref[:, (g * 2 + 1) * LANES:(g * 2 + 2) * LANES] = rep
    vbt_ref[...] = qkv_b[:, 1280:1536].T.astype(jnp.bfloat16)

    gate_ref[...] = jnp.dot(h, w_in_ref[:, 2048:4096], preferred_element_type=jnp.float32)


def _input_projection(x2d, norm1_g, w_in_p, gq, wq_p, gkv, wkv_p, cos, sin, S):
    N = x2d.shape[0]
    tm = TM_PROJ
    s_tiles = S // tm
    row = lambda i: (i, 0)
    tab = lambda i: (i % s_tiles, 0)
    bf = jnp.bfloat16
    col = lambda i: (0, i)
    kvw = KV_B * HD_B
    outs = [((N, H_A * QK_PAD), (tm, H_A * QK_PAD), row, bf),
            ((N, H_A * QK_PAD), (tm, H_A * QK_PAD), row, bf),
            ((H_A * V_DIM, N), (H_A * V_DIM, tm), col, bf),
            ((N, H_B * HD_B), (tm, H_B * HD_B), row, bf),
            ((N, GROUP * kvw), (tm, GROUP * kvw), row, bf),
            ((kvw, N), (kvw, tm), col, bf),
            ((N, 2 * D_MODEL), (tm, 2 * D_MODEL), row, jnp.float32)]
    return pl.pallas_call(
        functools.partial(_proj_kernel, scale_a=LOG2E / math.sqrt(QK_NOPE + QK_ROPE),
                          scale_b=LOG2E / math.sqrt(HD_B)),
        grid=(N // tm,),
        in_specs=[pl.BlockSpec((tm, D_MODEL), row),
                  _const_spec((1, D_MODEL)),
                  _const_spec(w_in_p.shape),
                  _const_spec((1, Q_LORA)),
                  _const_spec(wq_p.shape),
                  _const_spec((1, KV_LORA)),
                  _const_spec(wkv_p.shape),
                  pl.BlockSpec((tm, LANES), tab),
                  pl.BlockSpec((tm, LANES), tab)],
        out_specs=[pl.BlockSpec(blk, im) for _, blk, im, _ in outs],
        out_shape=[jax.ShapeDtypeStruct(shp, dt) for shp, _, _, dt in outs],
        compiler_params=_params(("parallel",)),
        name="input_projection",
    )(x2d, norm1_g, w_in_p, gq, wq_p, gkv, wkv_p, cos, sin)


def _mla_kernel(q_ref, k_ref, vt_ref, o_ref):
    q = q_ref[0]
    n_chunks = k_ref.shape[1] // TK_MLA

    def scores(c):
        kc = k_ref[0, c * TK_MLA:(c + 1) * TK_MLA, :]
        return lax.dot_general(kc, q, (((1,), (1,)), ((), ())), preferred_element_type=jnp.float32)

    m = l = acc = None
    ahead = [scores(c) for c in range(min(MLA_LOOKAHEAD, n_chunks))]
    pending = None
    for c in range(n_chunks + 1):
        if c < n_chunks:
            s = ahead.pop(0)
            if c + MLA_LOOKAHEAD < n_chunks:
                ahead.append(scores(c + MLA_LOOKAHEAD))
        if pending is not None:
            pc, pp, palpha = pending
            pv = jnp.dot(vt_ref[:, pc * TK_MLA:(pc + 1) * TK_MLA], pp,
                         preferred_element_type=jnp.float32)
            acc = pv if palpha is None else palpha * acc + pv
            pending = None
        if c < n_chunks:
            mc = jnp.max(s, axis=0, keepdims=True)
            m_new = mc if c == 0 else jnp.maximum(m, mc)
            p = jnp.exp2(s - m_new)
            ps = jnp.sum(p, axis=0, keepdims=True)
            alpha = None if c == 0 else jnp.exp2(m - m_new)
            l = ps if c == 0 else alpha * l + ps
            pending = (c, p.astype(jnp.bfloat16), alpha)
            m = m_new
    o_ref[0] = (acc / l).T.astype(o_ref.dtype)


def _mla_attention(qa, ka, vat):
    B, S, _ = qa.shape
    tq = TQ_MLA
    return pl.pallas_call(
        _mla_kernel,
        grid=(B, H_A, S // tq),
        in_specs=[pl.BlockSpec((1, tq, QK_PAD), lambda b, h, i: (b, i, h)),
                  pl.BlockSpec((1, S, QK_PAD), lambda b, h, i: (b, 0, h)),
                  pl.BlockSpec((V_DIM, S), lambda b, h, i: (h, b))],
        out_specs=pl.BlockSpec((1, tq, V_DIM), lambda b, h, i: (b, i, h)),
        out_shape=jax.ShapeDtypeStruct((B, S, H_A * V_DIM), jnp.float32),
        compiler_params=_params(("parallel", "parallel", "arbitrary")),
        name="mla_attention",
    )(qa, ka, vat)


def _window_kernel(q_ref, kp_ref, kc_ref, kn_ref, vp_ref, vc_ref, vn_ref, bias_ref, sink_ref, o_ref, ot_ref,
                   *, n_steps):
    step = pl.program_id(1)
    gw = GROUP * HD_B
    k_all = jnp.concatenate([kp_ref[0], kc_ref[0], kn_ref[0]], axis=0)
    vt_all = jnp.concatenate([vp_ref[...], vc_ref[...], vn_ref[...]], axis=1)
    lane = lax.broadcasted_iota(jnp.int32, (Q_BLOCK, gw), 1)
    for i in range(NQ_WIN):
        q_blk = q_ref[0, i * Q_BLOCK:(i + 1) * Q_BLOCK, :]
        for g in range(KV_B):
            qg = q_blk[:, g * gw:(g + 1) * gw]
            wt = jnp.concatenate(
                [jnp.where((lane >= j * HD_B) & (lane < (j + 1) * HD_B), qg, jnp.zeros_like(qg))
                 for j in range(GROUP)], axis=0)
            kg = k_all[i * Q_BLOCK:i * Q_BLOCK + SPAN, g * gw:(g + 1) * gw]
            s = lax.dot_general(kg, wt, (((1,), (1,)), ((), ())),
                                preferred_element_type=jnp.float32)
            parts = []
            for part in range(3):
                sp = s[part * Q_BLOCK:(part + 1) * Q_BLOCK] + bias_ref[g, part]
                if part == 0 and i == 0:
                    sp = jnp.where(step > 0, sp, NEG_INF)
                if part == 2 and i == NQ_WIN - 1:
                    sp = jnp.where(step < n_steps - 1, sp, NEG_INF)
                parts.append(sp)
            s = jnp.concatenate(parts, axis=0)
            sink = sink_ref[g]
            m = jnp.maximum(jnp.max(s, axis=0, keepdims=True), sink)
            p = jnp.exp2(s - m)
            l = jnp.sum(p, axis=0, keepdims=True) + jnp.exp2(sink - m)
            vt = vt_all[g * HD_B:(g + 1) * HD_B, i * Q_BLOCK:i * Q_BLOCK + SPAN]
            r = jnp.dot(vt, p.astype(jnp.bfloat16), preferred_element_type=jnp.float32) / l
            for j in range(GROUP):
                hd = g * GROUP + j
                ot_ref[hd * HD_B:(hd + 1) * HD_B, i * Q_BLOCK:(i + 1) * Q_BLOCK] = (
                    r[:, j * Q_BLOCK:(j + 1) * Q_BLOCK])
    o_ref[0] = ot_ref[...].T.astype(o_ref.dtype)


def _window_attention(qb, kb_rep, vbt, bias, sink_rows):
    B, S, _ = qb.shape
    nblk = S // Q_BLOCK
    rows = NQ_WIN * Q_BLOCK
    n_steps = S // rows
    w = H_B * HD_B
    kprev = lambda b, s: (b, jnp.maximum(s * NQ_WIN - 1, 0), 0)
    cur = lambda b, s: (b, s, 0)
    knext = lambda b, s: (b, jnp.minimum((s + 1) * NQ_WIN, nblk - 1), 0)
    vprev = lambda b, s: (0, b * nblk + jnp.maximum(s * NQ_WIN - 1, 0))
    vcur = lambda b, s: (0, b * n_steps + s)
    vnext = lambda b, s: (0, b * nblk + jnp.minimum((s + 1) * NQ_WIN, nblk - 1))
    kvw = KV_B * HD_B
    return pl.pallas_call(
        functools.partial(_window_kernel, n_steps=n_steps),
        grid=(B, n_steps),
        in_specs=[pl.BlockSpec((1, rows, w), cur),
                  pl.BlockSpec((1, Q_BLOCK, w), kprev),
                  pl.BlockSpec((1, rows, w), cur),
                  pl.BlockSpec((1, Q_BLOCK, w), knext),
                  pl.BlockSpec((kvw, Q_BLOCK), vprev),
                  pl.BlockSpec((kvw, rows), vcur),
                  pl.BlockSpec((kvw, Q_BLOCK), vnext),
                  _const_spec(bias.shape),
                  _const_spec(sink_rows.shape)],
        out_specs=pl.BlockSpec((1, rows, w), cur),
        out_shape=jax.ShapeDtypeStruct((B, S, w), jnp.float32),
        scratch_shapes=[pltpu.VMEM((w, rows), jnp.float32)],
        compiler_params=_params(("parallel", "arbitrary")),
        name="window_attention",
    )(qb, kb_rep, kb_rep, kb_rep, vbt, vbt, vbt, bias, sink_rows)


def _out_kernel(x_ref, oa_ref, ob_ref, gate_ref, w_ref, g2_ref, x1_ref, h2_ref):
    ga = gate_ref[:, 0:D_MODEL]
    gb = gate_ref[:, D_MODEL:2 * D_MODEL]
    mixed = jax.nn.sigmoid(ga) * oa_ref[...] + jax.nn.sigmoid(gb) * ob_ref[...]
    x1 = x_ref[...] + jnp.dot(mixed.astype(jnp.bfloat16), w_ref[...], preferred_element_type=jnp.float32)
    x1_ref[...] = x1
    h2_ref[...] = _rms(x1, g2_ref[...]).astype(jnp.bfloat16)


def _output_projection(x2d, oa, ob, gates, w_out, norm2_g):
    N = x2d.shape[0]
    tm = TM_OUT
    row = lambda i: (i, 0)
    return pl.pallas_call(
        _out_kernel,
        grid=(N // tm,),
        in_specs=[pl.BlockSpec((tm, D_MODEL), row),
                  pl.BlockSpec((tm, D_MODEL), row),
                  pl.BlockSpec((tm, D_MODEL), row),
                  pl.BlockSpec((tm, 2 * D_MODEL), row),
                  _const_spec(w_out.shape),
                  _const_spec((1, D_MODEL))],
        out_specs=[pl.BlockSpec((tm, D_MODEL), row), pl.BlockSpec((tm, D_MODEL), row)],
        out_shape=[jax.ShapeDtypeStruct((N, D_MODEL), jnp.float32),
                   jax.ShapeDtypeStruct((N, D_MODEL), jnp.bfloat16)],
        compiler_params=_params(("parallel",)),
        name="output_projection",
    )(x2d, oa, ob, gates, w_out, norm2_g)


def _ffn_kernel(x1_ref, hp_ref, hc_ref, hn_ref, wup_ref, cw_ref, cb_ref, wdn_ref, gf_ref, o_ref, acc_ref,
                *, n_tiles, final_norm):
    i = pl.program_id(1)
    tm = hc_ref.shape[1]
    hp = jnp.where(i > 0, hp_ref[0], jnp.zeros_like(hp_ref[0]))
    hn = jnp.where(i < n_tiles - 1, hn_ref[0], jnp.zeros_like(hn_ref[0]))
    he = jnp.concatenate([hp, hc_ref[0], hn], axis=0)
    acc_ref[...] = jnp.zeros_like(acc_ref)
    n_chunks = D_FF // FF_CHUNK

    def body(j, carry):
        c0 = pl.multiple_of(j * (2 * FF_CHUNK), 2 * FF_CHUNK)
        u = jnp.dot(he, wup_ref[:, pl.ds(c0, 2 * FF_CHUNK)], preferred_element_type=jnp.float32)
        cw = cw_ref[:, pl.ds(c0, 2 * FF_CHUNK)]
        cb = cb_ref[:, pl.ds(c0, 2 * FF_CHUNK)]
        y = (u[HALO - 1:HALO - 1 + tm] * cw[0:1] + u[HALO:HALO + tm] * cw[1:2]
             + u[HALO + 1:HALO + 1 + tm] * cw[2:3] + cb)
        gate = y[:, 0:FF_CHUNK]
        val = y[:, FF_CHUNK:2 * FF_CHUNK]
        act = (jax.nn.silu(gate) * val).astype(jnp.bfloat16)
        r0 = pl.multiple_of(j * FF_CHUNK, FF_CHUNK)
        acc_ref[...] += jnp.dot(act, wdn_ref[pl.ds(r0, FF_CHUNK), :], preferred_element_type=jnp.float32)
        return carry

    lax.fori_loop(0, n_chunks, body, 0)
    x2 = x1_ref[0] + acc_ref[...]
    o_ref[0] = _rms(x2, gf_ref[...]) if final_norm else x2


def _conv_ffn(x1, h2, wup_r, cw_r, cb_r, wdn, final_g, final_norm):
    B, S, _ = x1.shape
    tm = TM_FFN
    n_tiles = S // tm
    hb = tm // HALO
    cur = lambda b, i: (b, i, 0)
    prev = lambda b, i: (b, jnp.maximum(i * hb - 1, 0), 0)
    nxt = lambda b, i: (b, jnp.minimum((i + 1) * hb, S // HALO - 1), 0)
    return pl.pallas_call(
        functools.partial(_ffn_kernel, n_tiles=n_tiles, final_norm=final_norm),
        grid=(B, n_tiles),
        in_specs=[pl.BlockSpec((1, tm, D_MODEL), cur),
                  pl.BlockSpec((1, HALO, D_MODEL), prev),
                  pl.BlockSpec((1, tm, D_MODEL), cur),
                  pl.BlockSpec((1, HALO, D_MODEL), nxt),
                  _const_spec(wup_r.shape),
                  _const_spec(cw_r.shape),
                  _const_spec(cb_r.shape),
                  _const_spec(wdn.shape),
                  _const_spec((1, D_MODEL))],
        out_specs=pl.BlockSpec((1, tm, D_MODEL), cur),
        out_shape=jax.ShapeDtypeStruct((B, S, D_MODEL), jnp.float32),
        scratch_shapes=[pltpu.VMEM((tm, D_MODEL), jnp.float32)],
        compiler_params=_params(("parallel", "arbitrary")),
        name="conv_ffn",
    )(x1, h2, h2, h2, wup_r, cw_r, cb_r, wdn, final_g)


def _prep_w_in(w):
    pad = jnp.zeros((D_MODEL, LANES - QK_ROPE), w.dtype)
    split = Q_LORA + KV_LORA + QK_ROPE
    return jnp.concatenate([w[:, :split], pad, w[:, split:]], axis=1).astype(jnp.bfloat16)


def _prep_w_q_b(w):
    w3 = w.reshape(Q_LORA, H_A, QK_NOPE + QK_ROPE)
    pad = jnp.zeros((Q_LORA, H_A, QK_PAD - QK_NOPE - QK_ROPE), w.dtype)
    return jnp.concatenate([w3, pad], axis=-1).reshape(Q_LORA, H_A * QK_PAD).astype(jnp.bfloat16)


def _prep_w_kv_b(w):
    w3 = w.reshape(KV_LORA, H_A, QK_NOPE + V_DIM)
    k = w3[:, :, :QK_NOPE].reshape(KV_LORA, H_A * QK_NOPE)
    v = w3[:, :, QK_NOPE:].reshape(KV_LORA, H_A * V_DIM)
    return jnp.concatenate([k, v], axis=1).astype(jnp.bfloat16)


def _interleave_ff(a):
    lead = a.shape[:-1]
    n = D_FF // FF_CHUNK
    a = a.reshape(*lead, 2, n, FF_CHUNK)
    a = jnp.swapaxes(a, -3, -2)
    return a.reshape(*lead, 2 * D_FF)


def kernel(x, positions, norm1_g, w_in, q_a_norm_g, w_q_b, kv_a_norm_g, w_kv_b, rel_bias, sinks,
           w_out, norm2_g, w_up, conv_w, conv_b, w_down, final_norm_g):
    B, S, D = x.shape
    depth = norm1_g.shape[0]
    cos, sin = _rope_tables(positions)
    bias = _window_bias(rel_bias)
    f32 = jnp.float32
    for l in range(depth):
        x2d = x.reshape(B * S, D)
        qa, ka, vat, qb, kb_rep, vbt, gates = _input_projection(
            x2d, norm1_g[l][None].astype(f32), _prep_w_in(w_in[l]),
            q_a_norm_g[l][None].astype(f32), _prep_w_q_b(w_q_b[l]),
            kv_a_norm_g[l][None].astype(f32), _prep_w_kv_b(w_kv_b[l]), cos, sin, S)
        r3 = lambda a: a.reshape(B, S, a.shape[-1])
        oa = _mla_attention(r3(qa), r3(ka), vat)
        sink_rows = jnp.repeat(sinks[l].astype(f32) * LOG2E, Q_BLOCK).reshape(KV_B, 1, GROUP * Q_BLOCK)
        ob = _window_attention(r3(qb), r3(kb_rep), vbt, bias, sink_rows)
        x1, h2 = _output_projection(x2d, oa.reshape(B * S, D), ob.reshape(B * S, D), gates,
                                    w_out[l].astype(jnp.bfloat16), norm2_g[l][None].astype(f32))
        x = _conv_ffn(x1.reshape(B, S, D), h2.reshape(B, S, D),
                      _interleave_ff(w_up[l]).astype(jnp.bfloat16),
                      _interleave_ff(conv_w[l]).astype(f32),
                      _interleave_ff(conv_b[l])[None].astype(f32),
                      w_down[l].astype(jnp.bfloat16), final_norm_g[None].astype(f32),
                      final_norm=(l == depth - 1))
    return x
```

```python
import functools
import math

import jax
import jax.numpy as jnp
import numpy as np
from jax import lax
from jax.experimental import pallas as pl
from jax.experimental.pallas import tpu as pltpu

D_MODEL = 1024
EPS = 1e-6
Q_BLOCK = 128
H_A = 8
QK_NOPE = 128
QK_ROPE = 64
V_DIM = 128
Q_LORA = 256
KV_LORA = 128
ROPE_THETA = 10000.0
H_B = 16
KV_B = 4
GROUP = H_B // KV_B
HD_B = 64
WINDOW = 128
NUM_BUCKETS = 32
MAX_DISTANCE = 128
D_FF = 2816
SPAN = Q_BLOCK + 2 * WINDOW

LANES = 128
QK_PAD = 256
NEG_INF = -1e30
LOG2E = math.log2(math.e)
MIX_DTYPE = jnp.bfloat16
VMEM_LIMIT = 56 * 1024 * 1024

TM_PROJ = 512
TQ_MLA = 512
TK_MLA = 1024
MLA_LOOKAHEAD = 1
NQ_WIN = 4
TM_OUT = 512
TM_FFN = 512
FF_CHUNK = 256
HALO = 16


def _const_spec(shape):
    nd = len(shape)
    return pl.BlockSpec(shape, lambda *_: (0,) * nd, pipeline_mode=pl.Buffered(1))


def _params(sem):
    return pltpu.CompilerParams(dimension_semantics=sem, vmem_limit_bytes=VMEM_LIMIT)


def _rms(x, g):
    return x * lax.rsqrt(jnp.mean(x * x, axis=-1, keepdims=True) + EPS) * g


def _swap_halves_32(x):
    lane = lax.broadcasted_iota(jnp.int32, x.shape, 1)
    up = pltpu.roll(x, LANES - 32, 1)
    down = pltpu.roll(x, 32, 1)
    return jnp.where((lane & 32) == 0, up, down)


def _rope_table_kernel(pos_ref, inv_ref, sign_ref, cos_ref, sin_ref):
    ang = pos_ref[...] * inv_ref[...]
    cos_ref[...] = jnp.cos(ang)
    sin_ref[...] = jnp.sin(ang) * sign_ref[...]


def _rope_tables(positions):
    S = positions.shape[0]
    half = QK_ROPE // 2
    inv_freq = ROPE_THETA ** (-jnp.arange(half, dtype=jnp.float32) / half)
    inv = jnp.tile(inv_freq, LANES // half)[None, :]
    sign = jnp.tile(jnp.concatenate([-jnp.ones(half, jnp.float32), jnp.ones(half, jnp.float32)]),
                    LANES // QK_ROPE)[None, :]
    pos = positions.astype(jnp.float32)[:, None]
    ts = 512
    return pl.pallas_call(
        _rope_table_kernel,
        grid=(S // ts,),
        in_specs=[pl.BlockSpec((ts, 1), lambda i: (i, 0)),
                  pl.BlockSpec((1, LANES), lambda i: (0, 0)),
                  pl.BlockSpec((1, LANES), lambda i: (0, 0))],
        out_specs=[pl.BlockSpec((ts, LANES), lambda i: (i, 0)),
                   pl.BlockSpec((ts, LANES), lambda i: (i, 0))],
        out_shape=[jax.ShapeDtypeStruct((S, LANES), jnp.float32)] * 2,
        compiler_params=_params(("parallel",)),
        name="rope_tables",
    )(pos, inv, sign)


def _t5_bucket(rel):
    nb = NUM_BUCKETS // 2
    max_exact = nb // 2
    base = (rel > 0).astype(jnp.int32) * nb
    n = jnp.abs(rel)
    nf = jnp.maximum(n, 1).astype(jnp.float32)
    large = max_exact + (jnp.log(nf / max_exact) / math.log(MAX_DISTANCE / max_exact)
                         * (nb - max_exact)).astype(jnp.int32)
    large = jnp.minimum(large, nb - 1)
    return base + jnp.where(n < max_exact, n, large)


def _bias_kernel(rel_bias_ref, bucket_ref, band_ref, out_ref):
    g = pl.program_id(0)
    bucket = bucket_ref[0]
    band = band_ref[0]
    for j in range(GROUP):
        acc = jnp.zeros(bucket.shape, jnp.float32)
        for b in range(NUM_BUCKETS):
            acc = jnp.where(bucket == b, rel_bias_ref[b, g * GROUP + j], acc)
        out_ref[0, 0, :, j * Q_BLOCK:(j + 1) * Q_BLOCK] = jnp.where(band != 0, acc * LOG2E, NEG_INF)


def _window_bias(rel_bias):
    a = jnp.arange(Q_BLOCK, dtype=jnp.int32)[None, :]
    c = jnp.arange(SPAN, dtype=jnp.int32)[:, None]
    rel = c - WINDOW - a
    bucket = _t5_bucket(rel).reshape(3, Q_BLOCK, Q_BLOCK)
    band = (jnp.abs(rel) <= WINDOW).astype(jnp.int32).reshape(3, Q_BLOCK, Q_BLOCK)
    part = pl.BlockSpec((1, Q_BLOCK, Q_BLOCK), lambda g, p: (p, 0, 0))
    return pl.pallas_call(
        _bias_kernel,
        grid=(KV_B, 3),
        in_specs=[pl.BlockSpec(memory_space=pltpu.SMEM), part, part],
        out_specs=pl.BlockSpec((1, 1, Q_BLOCK, GROUP * Q_BLOCK), lambda g, p: (g, p, 0, 0)),
        out_shape=jax.ShapeDtypeStruct((KV_B, 3, Q_BLOCK, GROUP * Q_BLOCK), jnp.float32),
        compiler_params=_params(("arbitrary", "arbitrary")),
        name="window_bias",
    )(rel_bias.astype(jnp.float32), bucket, band)


def _proj_kernel(x_ref, g1_ref, w_in_ref, gq_ref, wq_ref, gkv_ref, wkv_ref, cos_ref, sin_ref,
                 qa_ref, ka_ref, vat_ref, qb_ref, kb_ref, vbt_ref, gate_ref, *, scale_a, scale_b):
    x = x_ref[...]
    h = _rms(x, g1_ref[...]).astype(jnp.bfloat16)
    cos = cos_ref[...]
    sin = sin_ref[...]

    def rope(blk):
        return blk * cos + _swap_halves_32(blk) * sin

    lat = jnp.dot(h, w_in_ref[:, 0:512], preferred_element_type=jnp.float32)
    q_lat = lat[:, 0:Q_LORA]
    c_kv = lat[:, Q_LORA:Q_LORA + KV_LORA]
    k_rope = rope(lat[:, 384:512]).astype(jnp.bfloat16)

    qn = _rms(q_lat, gq_ref[...]).astype(jnp.bfloat16)
    q = jnp.dot(qn, wq_ref[...], preferred_element_type=jnp.float32)
    for hd in range(H_A):
        base = hd * QK_PAD
        qa_ref[:, base:base + QK_NOPE] = (q[:, base:base + QK_NOPE] * scale_a).astype(jnp.bfloat16)
        qa_ref[:, base + QK_NOPE:base + QK_PAD] = (
            rope(q[:, base + QK_NOPE:base + QK_PAD]) * scale_a).astype(jnp.bfloat16)

    cn = _rms(c_kv, gkv_ref[...]).astype(jnp.bfloat16)
    kv = jnp.dot(cn, wkv_ref[...], preferred_element_type=jnp.float32)
    for hd in range(H_A):
        base = hd * QK_PAD
        ka_ref[:, base:base + QK_NOPE] = kv[:, hd * QK_NOPE:(hd + 1) * QK_NOPE].astype(jnp.bfloat16)
        ka_ref[:, base + QK_NOPE:base + QK_PAD] = k_rope
    vat_ref[...] = kv[:, H_A * QK_NOPE:].T.astype(jnp.bfloat16)

    qkv_b = jnp.dot(h, w_in_ref[:, 512:2048], preferred_element_type=jnp.float32)
    qb_ref[...] = (qkv_b[:, 0:1024] * scale_b).astype(jnp.bfloat16)
    lane = lax.broadcasted_iota(jnp.int32, (x.shape[0], LANES), 1)
    for g in range(KV_B):
        pair = qkv_b[:, 1024 + (g // 2) * LANES:1024 + (g // 2 + 1) * LANES]
        other = pltpu.roll(pair, HD_B, 1)
        first = (lane < HD_B) if g % 2 == 0 else (lane >= HD_B)
        rep = jnp.where(first, pair, other).astype(jnp.bfloat16)
        kb_ref[:, g * 2 * LANES:(g * 2 + 1) * LANES] = rep
        kb_ref[:, (g * 2 + 1) * LANES:(g * 2 + 2) * LANES] = rep
    vbt_ref[...] = qkv_b[:, 1280:1536].T.astype(jnp.bfloat16)

    gate_ref[...] = jnp.dot(h, w_in_ref[:, 2048:4096],
                            preferred_element_type=jnp.float32).astype(gate_ref.dtype)


def _input_projection(x2d, norm1_g, w_in_p, gq, wq_p, gkv, wkv_p, cos, sin, S):
    N = x2d.shape[0]
    tm = TM_PROJ
    s_tiles = S // tm
    row = lambda i: (i, 0)
    tab = lambda i: (i % s_tiles, 0)
    bf = jnp.bfloat16
    col = lambda i: (0, i)
    kvw = KV_B * HD_B
    outs = [((N, H_A * QK_PAD), (tm, H_A * QK_PAD), row, bf),
            ((N, H_A * QK_PAD), (tm, H_A * QK_PAD), row, bf),
            ((H_A * V_DIM, N), (H_A * V_DIM, tm), col, bf),
            ((N, H_B * HD_B), (tm, H_B * HD_B), row, bf),
            ((N, GROUP * kvw), (tm, GROUP * kvw), row, bf),
            ((kvw, N), (kvw, tm), col, bf),
            ((N, 2 * D_MODEL), (tm, 2 * D_MODEL), row, MIX_DTYPE)]
    return pl.pallas_call(
        functools.partial(_proj_kernel, scale_a=LOG2E / math.sqrt(QK_NOPE + QK_ROPE),
                          scale_b=LOG2E / math.sqrt(HD_B)),
        grid=(N // tm,),
        in_specs=[pl.BlockSpec((tm, D_MODEL), row),
                  _const_spec((1, D_MODEL)),
                  _const_spec(w_in_p.shape),
                  _const_spec((1, Q_LORA)),
                  _const_spec(wq_p.shape),
                  _const_spec((1, KV_LORA)),
                  _const_spec(wkv_p.shape),
                  pl.BlockSpec((tm, LANES), tab),
                  pl.BlockSpec((tm, LANES), tab)],
        out_specs=[pl.BlockSpec(blk, im) for _, blk, im, _ in outs],
        out_shape=[jax.ShapeDtypeStruct(shp, dt) for shp, _, _, dt in outs],
        compiler_params=_params(("parallel",)),
        name="input_projection",
    )(x2d, norm1_g, w_in_p, gq, wq_p, gkv, wkv_p, cos, sin)


def _mla_kernel(q_ref, k_ref, vt_ref, o_ref):
    q = q_ref[0]
    n_chunks = k_ref.shape[1] // TK_MLA

    def scores(c):
        kc = k_ref[0, c * TK_MLA:(c + 1) * TK_MLA, :]
        return lax.dot_general(kc, q, (((1,), (1,)), ((), ())), preferred_element_type=jnp.float32)

    m = l = acc = None
    ahead = [scores(c) for c in range(min(MLA_LOOKAHEAD, n_chunks))]
    pending = None
    for c in range(n_chunks + 1):
        if c < n_chunks:
            s = ahead.pop(0)
            if c + MLA_LOOKAHEAD < n_chunks:
                ahead.append(scores(c + MLA_LOOKAHEAD))
        if pending is not None:
            pc, pp, palpha = pending
            pv = jnp.dot(vt_ref[:, pc * TK_MLA:(pc + 1) * TK_MLA], pp,
                         preferred_element_type=jnp.float32)
            acc = pv if palpha is None else palpha * acc + pv
            pending = None
        if c < n_chunks:
            mc = jnp.max(s, axis=0, keepdims=True)
            m_new = mc if c == 0 else jnp.maximum(m, mc)
            p = jnp.exp2(s - m_new)
            ps = jnp.sum(p, axis=0, keepdims=True)
            alpha = None if c == 0 else jnp.exp2(m - m_new)
            l = ps if c == 0 else alpha * l + ps
            pending = (c, p.astype(jnp.bfloat16), alpha)
            m = m_new
    o_ref[0] = (acc / l).T.astype(o_ref.dtype)


def _mla_attention(qa, ka, vat):
    B, S, _ = qa.shape
    tq = TQ_MLA
    return pl.pallas_call(
        _mla_kernel,
        grid=(B, H_A, S // tq),
        in_specs=[pl.BlockSpec((1, tq, QK_PAD), lambda b, h, i: (b, i, h)),
                  pl.BlockSpec((1, S, QK_PAD), lambda b, h, i: (b, 0, h)),
                  pl.BlockSpec((V_DIM, S), lambda b, h, i: (h, b))],
        out_specs=pl.BlockSpec((1, tq, V_DIM), lambda b, h, i: (b, i, h)),
        out_shape=jax.ShapeDtypeStruct((B, S, H_A * V_DIM), MIX_DTYPE),
        compiler_params=_params(("parallel", "parallel", "arbitrary")),
        name="mla_attention",
    )(qa, ka, vat)


def _window_kernel(q_ref, kp_ref, kc_ref, kn_ref, vp_ref, vc_ref, vn_ref, bias_ref, sink_ref, o_ref, ot_ref,
                   *, n_steps):
    step = pl.program_id(1)
    gw = GROUP * HD_B
    k_all = jnp.concatenate([kp_ref[0], kc_ref[0], kn_ref[0]], axis=0)
    vt_all = jnp.concatenate([vp_ref[...], vc_ref[...], vn_ref[...]], axis=1)
    lane = lax.broadcasted_iota(jnp.int32, (Q_BLOCK, gw), 1)
    for i in range(NQ_WIN):
        q_blk = q_ref[0, i * Q_BLOCK:(i + 1) * Q_BLOCK, :]
        for g in range(KV_B):
            qg = q_blk[:, g * gw:(g + 1) * gw]
            wt = jnp.concatenate(
                [jnp.where((lane >= j * HD_B) & (lane < (j + 1) * HD_B), qg, jnp.zeros_like(qg))
                 for j in range(GROUP)], axis=0)
            kg = k_all[i * Q_BLOCK:i * Q_BLOCK + SPAN, g * gw:(g + 1) * gw]
            s = lax.dot_general(kg, wt, (((1,), (1,)), ((), ())),
                                preferred_element_type=jnp.float32)
            parts = []
            for part in range(3):
                sp = s[part * Q_BLOCK:(part + 1) * Q_BLOCK] + bias_ref[g, part]
                if part == 0 and i == 0:
                    sp = jnp.where(step > 0, sp, NEG_INF)
                if part == 2 and i == NQ_WIN - 1:
                    sp = jnp.where(step < n_steps - 1, sp, NEG_INF)
                parts.append(sp)
            s = jnp.concatenate(parts, axis=0)
            sink = sink_ref[g]
            m = jnp.maximum(jnp.max(s, axis=0, keepdims=True), sink)
            p = jnp.exp2(s - m)
            l = jnp.sum(p, axis=0, keepdims=True) + jnp.exp2(sink - m)
            vt = vt_all[g * HD_B:(g + 1) * HD_B, i * Q_BLOCK:i * Q_BLOCK + SPAN]
            r = jnp.dot(vt, p.astype(jnp.bfloat16), preferred_element_type=jnp.float32) / l
            for j in range(GROUP):
                hd = g * GROUP + j
                ot_ref[hd * HD_B:(hd + 1) * HD_B, i * Q_BLOCK:(i + 1) * Q_BLOCK] = (
                    r[:, j * Q_BLOCK:(j + 1) * Q_BLOCK])
    o_ref[0] = ot_ref[...].T.astype(o_ref.dtype)


def _window_attention(qb, kb_rep, vbt, bias, sink_rows):
    B, S, _ = qb.shape
    nblk = S // Q_BLOCK
    rows = NQ_WIN * Q_BLOCK
    n_steps = S // rows
    w = H_B * HD_B
    kprev = lambda b, s: (b, jnp.maximum(s * NQ_WIN - 1, 0), 0)
    cur = lambda b, s: (b, s, 0)
    knext = lambda b, s: (b, jnp.minimum((s + 1) * NQ_WIN, nblk - 1), 0)
    vprev = lambda b, s: (0, b * nblk + jnp.maximum(s * NQ_WIN - 1, 0))
    vcur = lambda b, s: (0, b * n_steps + s)
    vnext = lambda b, s: (0, b * nblk + jnp.minimum((s + 1) * NQ_WIN, nblk - 1))
    kvw = KV_B * HD_B
    return pl.pallas_call(
        functools.partial(_window_kernel, n_steps=n_steps),
        grid=(B, n_steps),
        in_specs=[pl.BlockSpec((1, rows, w), cur),
                  pl.BlockSpec((1, Q_BLOCK, w), kprev),
                  pl.BlockSpec((1, rows, w), cur),
                  pl.BlockSpec((1, Q_BLOCK, w), knext),
                  pl.BlockSpec((kvw, Q_BLOCK), vprev),
                  pl.BlockSpec((kvw, rows), vcur),
                  pl.BlockSpec((kvw, Q_BLOCK), vnext),
                  _const_spec(bias.shape),
                  _const_spec(sink_rows.shape)],
        out_specs=pl.BlockSpec((1, rows, w), cur),
        out_shape=jax.ShapeDtypeStruct((B, S, w), MIX_DTYPE),
        scratch_shapes=[pltpu.VMEM((w, rows), jnp.float32)],
        compiler_params=_params(("parallel", "arbitrary")),
        name="window_attention",
    )(qb, kb_rep, kb_rep, kb_rep, vbt, vbt, vbt, bias, sink_rows)


def _out_kernel(x_ref, oa_ref, ob_ref, gate_ref, w_ref, g2_ref, x1_ref, h2_ref):
    f32 = jnp.float32
    ga = gate_ref[:, 0:D_MODEL].astype(f32)
    gb = gate_ref[:, D_MODEL:2 * D_MODEL].astype(f32)
    mixed = jax.nn.sigmoid(ga) * oa_ref[...].astype(f32) + jax.nn.sigmoid(gb) * ob_ref[...].astype(f32)
    x1 = x_ref[...] + jnp.dot(mixed.astype(jnp.bfloat16), w_ref[...], preferred_element_type=jnp.float32)
    x1_ref[...] = x1
    h2_ref[...] = _rms(x1, g2_ref[...]).astype(jnp.bfloat16)


def _output_projection(x2d, oa, ob, gates, w_out, norm2_g):
    N = x2d.shape[0]
    tm = TM_OUT
    row = lambda i: (i, 0)
    return pl.pallas_call(
        _out_kernel,
        grid=(N // tm,),
        in_specs=[pl.BlockSpec((tm, D_MODEL), row),
                  pl.BlockSpec((tm, D_MODEL), row),
                  pl.BlockSpec((tm, D_MODEL), row),
                  pl.BlockSpec((tm, 2 * D_MODEL), row),
                  _const_spec(w_out.shape),
                  _const_spec((1, D_MODEL))],
        out_specs=[pl.BlockSpec((tm, D_MODEL), row), pl.BlockSpec((tm, D_MODEL), row)],
        out_shape=[jax.ShapeDtypeStruct((N, D_MODEL), jnp.float32),
                   jax.ShapeDtypeStruct((N, D_MODEL), jnp.bfloat16)],
        compiler_params=_params(("parallel",)),
        name="output_projection",
    )(x2d, oa, ob, gates, w_out, norm2_g)


def _ffn_kernel(x1_ref, hp_ref, hc_ref, hn_ref, wup_ref, cw_ref, cb_ref, wdn_ref, gf_ref, o_ref, act_ref,
                *, n_tiles, final_norm):
    i = pl.program_id(1)
    tm = hc_ref.shape[1]
    hp = jnp.where(i > 0, hp_ref[0], jnp.zeros_like(hp_ref[0]))
    hn = jnp.where(i < n_tiles - 1, hn_ref[0], jnp.zeros_like(hn_ref[0]))
    he = jnp.concatenate([hp, hc_ref[0], hn], axis=0)
    n_chunks = D_FF // FF_CHUNK

    def cols(ref, j):
        g0 = j * FF_CHUNK
        return jnp.concatenate([ref[:, g0:g0 + FF_CHUNK], ref[:, D_FF + g0:D_FF + g0 + FF_CHUNK]], axis=1)

    def up(j):
        return jnp.dot(he, cols(wup_ref, j), preferred_element_type=jnp.float32)

    u_next = up(0)
    for j in range(n_chunks):
        u = u_next
        if j + 1 < n_chunks:
            u_next = up(j + 1)
        cw = cols(cw_ref, j)
        cb = cols(cb_ref, j)
        y = (u[HALO - 1:HALO - 1 + tm] * cw[0:1] + u[HALO:HALO + tm] * cw[1:2]
             + u[HALO + 1:HALO + 1 + tm] * cw[2:3] + cb)
        gate = y[:, 0:FF_CHUNK]
        val = y[:, FF_CHUNK:2 * FF_CHUNK]
        act_ref[:, j * FF_CHUNK:(j + 1) * FF_CHUNK] = (jax.nn.silu(gate) * val).astype(jnp.bfloat16)
    x2 = x1_ref[0] + jnp.dot(act_ref[...], wdn_ref[...], preferred_element_type=jnp.float32)
    o_ref[0] = _rms(x2, gf_ref[...]) if final_norm else x2


def _conv_ffn(x1, h2, wup_r, cw_r, cb_r, wdn, final_g, final_norm):
    B, S, _ = x1.shape
    tm = TM_FFN
    n_tiles = S // tm
    hb = tm // HALO
    cur = lambda b, i: (b, i, 0)
    prev = lambda b, i: (b, jnp.maximum(i * hb - 1, 0), 0)
    nxt = lambda b, i: (b, jnp.minimum((i + 1) * hb, S // HALO - 1), 0)
    return pl.pallas_call(
        functools.partial(_ffn_kernel, n_tiles=n_tiles, final_norm=final_norm),
        grid=(B, n_tiles),
        in_specs=[pl.BlockSpec((1, tm, D_MODEL), cur),
                  pl.BlockSpec((1, HALO, D_MODEL), prev),
                  pl.BlockSpec((1, tm, D_MODEL), cur),
                  pl.BlockSpec((1, HALO, D_MODEL), nxt),
                  _const_spec(wup_r.shape),
                  _const_spec(cw_r.shape),
                  _const_spec(cb_r.shape),
                  _const_spec(wdn.shape),
                  _const_spec((1, D_MODEL))],
        out_specs=pl.BlockSpec((1, tm, D_MODEL), cur),
        out_shape=jax.ShapeDtypeStruct((B, S, D_MODEL), jnp.float32),
        scratch_shapes=[pltpu.VMEM((tm, D_FF), jnp.bfloat16)],
        compiler_params=_params(("parallel", "arbitrary")),
        name="conv_ffn",
    )(x1, h2, h2, h2, wup_r, cw_r, cb_r, wdn, final_g)


def _prep_w_in(w):
    pad = jnp.zeros((D_MODEL, LANES - QK_ROPE), w.dtype)
    split = Q_LORA + KV_LORA + QK_ROPE
    return jnp.concatenate([w[:, :split], pad, w[:, split:]], axis=1).astype(jnp.bfloat16)


def _prep_w_q_b(w):
    w3 = w.reshape(Q_LORA, H_A, QK_NOPE + QK_ROPE)
    pad = jnp.zeros((Q_LORA, H_A, QK_PAD - QK_NOPE - QK_ROPE), w.dtype)
    return jnp.concatenate([w3, pad], axis=-1).reshape(Q_LORA, H_A * QK_PAD).astype(jnp.bfloat16)


def _prep_w_kv_b(w):
    w3 = w.reshape(KV_LORA, H_A, QK_NOPE + V_DIM)
    k = w3[:, :, :QK_NOPE].reshape(KV_LORA, H_A * QK_NOPE)
    v = w3[:, :, QK_NOPE:].reshape(KV_LORA, H_A * V_DIM)
    return jnp.concatenate([k, v], axis=1).astype(jnp.bfloat16)


def kernel(x, positions, norm1_g, w_in, q_a_norm_g, w_q_b, kv_a_norm_g, w_kv_b, rel_bias, sinks,
           w_out, norm2_g, w_up, conv_w, conv_b, w_down, final_norm_g):
    B, S, D = x.shape
    depth = norm1_g.shape[0]
    cos, sin = _rope_tables(positions)
    bias = _window_bias(rel_bias)
    f32 = jnp.float32
    for l in range(depth):
        x2d = x.reshape(B * S, D)
        qa, ka, vat, qb, kb_rep, vbt, gates = _input_projection(
            x2d, norm1_g[l][None].astype(f32), _prep_w_in(w_in[l]),
            q_a_norm_g[l][None].astype(f32), _prep_w_q_b(w_q_b[l]),
            kv_a_norm_g[l][None].astype(f32), _prep_w_kv_b(w_kv_b[l]), cos, sin, S)
        r3 = lambda a: a.reshape(B, S, a.shape[-1])
        oa = _mla_attention(r3(qa), r3(ka), vat)
        sink_rows = jnp.repeat(sinks[l].astype(f32) * LOG2E, Q_BLOCK).reshape(KV_B, 1, GROUP * Q_BLOCK)
        ob = _window_attention(r3(qb), r3(kb_rep), vbt, bias, sink_rows)
        x1, h2 = _output_projection(x2d, oa.reshape(B * S, D), ob.reshape(B * S, D), gates,
                                    w_out[l].astype(jnp.bfloat16), norm2_g[l][None].astype(f32))
        x = _conv_ffn(x1.reshape(B, S, D), h2.reshape(B, S, D),
                      w_up[l].astype(jnp.bfloat16), conv_w[l].astype(f32), conv_b[l][None].astype(f32),
                      w_down[l].astype(jnp.bfloat16), final_norm_g[None].astype(f32),
                      final_norm=(l == depth - 1))
    return x
```

```python
import functools
import math

import jax
import jax.numpy as jnp
import numpy as np
from jax import lax
from jax.experimental import pallas as pl
from jax.experimental.pallas import tpu as pltpu

D_MODEL = 1024
EPS = 1e-6
Q_BLOCK = 128
H_A = 8
QK_NOPE = 128
QK_ROPE = 64
V_DIM = 128
Q_LORA = 256
KV_LORA = 128
ROPE_THETA = 10000.0
H_B = 16
KV_B = 4
GROUP = H_B // KV_B
HD_B = 64
WINDOW = 128
NUM_BUCKETS = 32
MAX_DISTANCE = 128
D_FF = 2816
SPAN = Q_BLOCK + 2 * WINDOW

LANES = 128
QK_PAD = 256
NEG_INF = -1e30
LOG2E = math.log2(math.e)
MIX_DTYPE = jnp.bfloat16
VMEM_LIMIT = 56 * 1024 * 1024

TM_PROJ = 512
TQ_MLA = 1024
TK_MLA = 1024
MLA_LOOKAHEAD = 1
NQ_WIN = 4
TM_FFN = 512
FF_CHUNK = 256
HALO = 16


def _const_spec(shape):
    nd = len(shape)
    return pl.BlockSpec(shape, lambda *_: (0,) * nd, pipeline_mode=pl.Buffered(1))


def _params(sem):
    return pltpu.CompilerParams(dimension_semantics=sem, vmem_limit_bytes=VMEM_LIMIT)


def _rms(x, g):
    return x * lax.rsqrt(jnp.mean(x * x, axis=-1, keepdims=True) + EPS) * g


def _swap_halves_32(x):
    lane = lax.broadcasted_iota(jnp.int32, x.shape, 1)
    up = pltpu.roll(x, LANES - 32, 1)
    down = pltpu.roll(x, 32, 1)
    return jnp.where((lane & 32) == 0, up, down)


def _rope_table_kernel(pos_ref, inv_ref, sign_ref, cos_ref, sin_ref):
    ang = pos_ref[...] * inv_ref[...]
    cos_ref[...] = jnp.cos(ang)
    sin_ref[...] = jnp.sin(ang) * sign_ref[...]


def _rope_tables(positions):
    S = positions.shape[0]
    half = QK_ROPE // 2
    inv_freq = ROPE_THETA ** (-jnp.arange(half, dtype=jnp.float32) / half)
    inv = jnp.tile(inv_freq, LANES // half)[None, :]
    sign = jnp.tile(jnp.concatenate([-jnp.ones(half, jnp.float32), jnp.ones(half, jnp.float32)]),
                    LANES // QK_ROPE)[None, :]
    pos = positions.astype(jnp.float32)[:, None]
    ts = 512
    return pl.pallas_call(
        _rope_table_kernel,
        grid=(S // ts,),
        in_specs=[pl.BlockSpec((ts, 1), lambda i: (i, 0)),
                  pl.BlockSpec((1, LANES), lambda i: (0, 0)),
                  pl.BlockSpec((1, LANES), lambda i: (0, 0))],
        out_specs=[pl.BlockSpec((ts, LANES), lambda i: (i, 0)),
                   pl.BlockSpec((ts, LANES), lambda i: (i, 0))],
        out_shape=[jax.ShapeDtypeStruct((S, LANES), jnp.float32)] * 2,
        compiler_params=_params(("parallel",)),
        name="rope_tables",
    )(pos, inv, sign)


def _t5_bucket(rel):
    nb = NUM_BUCKETS // 2
    max_exact = nb // 2
    base = (rel > 0).astype(jnp.int32) * nb
    n = jnp.abs(rel)
    nf = jnp.maximum(n, 1).astype(jnp.float32)
    large = max_exact + (jnp.log(nf / max_exact) / math.log(MAX_DISTANCE / max_exact)
                         * (nb - max_exact)).astype(jnp.int32)
    large = jnp.minimum(large, nb - 1)
    return base + jnp.where(n < max_exact, n, large)


def _bias_kernel(rel_bias_ref, bucket_ref, band_ref, out_ref):
    g = pl.program_id(0)
    bucket = bucket_ref[0]
    band = band_ref[0]
    for j in range(GROUP):
        acc = jnp.zeros(bucket.shape, jnp.float32)
        for b in range(NUM_BUCKETS):
            acc = jnp.where(bucket == b, rel_bias_ref[b, g * GROUP + j], acc)
        out_ref[0, 0, :, j * Q_BLOCK:(j + 1) * Q_BLOCK] = jnp.where(band != 0, acc * LOG2E, NEG_INF)


def _window_bias(rel_bias):
    a = jnp.arange(Q_BLOCK, dtype=jnp.int32)[None, :]
    c = jnp.arange(SPAN, dtype=jnp.int32)[:, None]
    rel = c - WINDOW - a
    bucket = _t5_bucket(rel).reshape(3, Q_BLOCK, Q_BLOCK)
    band = (jnp.abs(rel) <= WINDOW).astype(jnp.int32).reshape(3, Q_BLOCK, Q_BLOCK)
    part = pl.BlockSpec((1, Q_BLOCK, Q_BLOCK), lambda g, p: (p, 0, 0))
    return pl.pallas_call(
        _bias_kernel,
        grid=(KV_B, 3),
        in_specs=[pl.BlockSpec(memory_space=pltpu.SMEM), part, part],
        out_specs=pl.BlockSpec((1, 1, Q_BLOCK, GROUP * Q_BLOCK), lambda g, p: (g, p, 0, 0)),
        out_shape=jax.ShapeDtypeStruct((KV_B, 3, Q_BLOCK, GROUP * Q_BLOCK), jnp.float32),
        compiler_params=_params(("arbitrary", "arbitrary")),
        name="window_bias",
    )(rel_bias.astype(jnp.float32), bucket, band)


def _proj_kernel(x_ref, g1_ref, w_in_ref, gq_ref, wq_ref, gkv_ref, wkv_ref, cos_ref, sin_ref,
                 qa_ref, ka_ref, vat_ref, qb_ref, kb_ref, vbt_ref, gate_ref, *, scale_a, scale_b):
    x = x_ref[...]
    h = _rms(x, g1_ref[...]).astype(jnp.bfloat16)
    cos = cos_ref[...]
    sin = sin_ref[...]

    def rope(blk):
        return blk * cos + _swap_halves_32(blk) * sin

    lat = jnp.dot(h, w_in_ref[:, 0:512], preferred_element_type=jnp.float32)
    q_lat = lat[:, 0:Q_LORA]
    c_kv = lat[:, Q_LORA:Q_LORA + KV_LORA]
    k_rope = rope(lat[:, 384:512]).astype(jnp.bfloat16)

    qn = _rms(q_lat, gq_ref[...]).astype(jnp.bfloat16)
    q = jnp.dot(qn, wq_ref[...], preferred_element_type=jnp.float32)
    for hd in range(H_A):
        base = hd * QK_PAD
        qa_ref[:, base:base + QK_NOPE] = (q[:, base:base + QK_NOPE] * scale_a).astype(jnp.bfloat16)
        qa_ref[:, base + QK_NOPE:base + QK_PAD] = (
            rope(q[:, base + QK_NOPE:base + QK_PAD]) * scale_a).astype(jnp.bfloat16)

    cn = _rms(c_kv, gkv_ref[...]).astype(jnp.bfloat16)
    kv = jnp.dot(cn, wkv_ref[...], preferred_element_type=jnp.float32)
    for hd in range(H_A):
        base = hd * QK_PAD
        ka_ref[:, base:base + QK_NOPE] = kv[:, hd * QK_NOPE:(hd + 1) * QK_NOPE].astype(jnp.bfloat16)
        ka_ref[:, base + QK_NOPE:base + QK_PAD] = k_rope
    vat_ref[...] = kv[:, H_A * QK_NOPE:].T.astype(jnp.bfloat16)

    qkv_b = jnp.dot(h, w_in_ref[:, 512:2048], preferred_element_type=jnp.float32)
    qb_ref[...] = (qkv_b[:, 0:1024] * scale_b).astype(jnp.bfloat16)
    lane = lax.broadcasted_iota(jnp.int32, (x.shape[0], LANES), 1)
    for g in range(KV_B):
        pair = qkv_b[:, 1024 + (g // 2) * LANES:1024 + (g // 2 + 1) * LANES]
        other = pltpu.roll(pair, HD_B, 1)
        first = (lane < HD_B) if g % 2 == 0 else (lane >= HD_B)
        rep = jnp.where(first, pair, other).astype(jnp.bfloat16)
        kb_ref[:, g * 2 * LANES:(g * 2 + 1) * LANES] = rep
        kb_ref[:, (g * 2 + 1) * LANES:(g * 2 + 2) * LANES] = rep
    vbt_ref[...] = qkv_b[:, 1280:1536].T.astype(jnp.bfloat16)

    gate_ref[...] = jnp.dot(h, w_in_ref[:, 2048:4096],
                            preferred_element_type=jnp.float32).astype(gate_ref.dtype)


def _input_projection(x2d, norm1_g, w_in_p, gq, wq_p, gkv, wkv_p, cos, sin, S):
    N = x2d.shape[0]
    tm = TM_PROJ
    s_tiles = S // tm
    row = lambda i: (i, 0)
    tab = lambda i: (i % s_tiles, 0)
    bf = jnp.bfloat16
    col = lambda i: (0, i)
    kvw = KV_B * HD_B
    outs = [((N, H_A * QK_PAD), (tm, H_A * QK_PAD), row, bf),
            ((N, H_A * QK_PAD), (tm, H_A * QK_PAD), row, bf),
            ((H_A * V_DIM, N), (H_A * V_DIM, tm), col, bf),
            ((N, H_B * HD_B), (tm, H_B * HD_B), row, bf),
            ((N, GROUP * kvw), (tm, GROUP * kvw), row, bf),
            ((kvw, N), (kvw, tm), col, bf),
            ((N, 2 * D_MODEL), (tm, 2 * D_MODEL), row, MIX_DTYPE)]
    return pl.pallas_call(
        functools.partial(_proj_kernel, scale_a=LOG2E / math.sqrt(QK_NOPE + QK_ROPE),
                          scale_b=LOG2E / math.sqrt(HD_B)),
        grid=(N // tm,),
        in_specs=[pl.BlockSpec((tm, D_MODEL), row),
                  _const_spec((1, D_MODEL)),
                  _const_spec(w_in_p.shape),
                  _const_spec((1, Q_LORA)),
                  _const_spec(wq_p.shape),
                  _const_spec((1, KV_LORA)),
                  _const_spec(wkv_p.shape),
                  pl.BlockSpec((tm, LANES), tab),
                  pl.BlockSpec((tm, LANES), tab)],
        out_specs=[pl.BlockSpec(blk, im) for _, blk, im, _ in outs],
        out_shape=[jax.ShapeDtypeStruct(shp, dt) for shp, _, _, dt in outs],
        compiler_params=_params(("parallel",)),
        name="input_projection",
    )(x2d, norm1_g, w_in_p, gq, wq_p, gkv, wkv_p, cos, sin)


def _mla_kernel(q_ref, k_ref, vt_ref, o_ref):
    q = q_ref[0]
    n_chunks = k_ref.shape[1] // TK_MLA

    def scores(c):
        kc = k_ref[0, c * TK_MLA:(c + 1) * TK_MLA, :]
        return lax.dot_general(kc, q, (((1,), (1,)), ((), ())), preferred_element_type=jnp.float32)

    m = l = acc = None
    ahead = [scores(c) for c in range(min(MLA_LOOKAHEAD, n_chunks))]
    pending = None
    for c in range(n_chunks + 1):
        if c < n_chunks:
            s = ahead.pop(0)
            if c + MLA_LOOKAHEAD < n_chunks:
                ahead.append(scores(c + MLA_LOOKAHEAD))
        if pending is not None:
            pc, pp, palpha = pending
            pv = jnp.dot(vt_ref[:, pc * TK_MLA:(pc + 1) * TK_MLA], pp,
                         preferred_element_type=jnp.float32)
            acc = pv if palpha is None else palpha * acc + pv
            pending = None
        if c < n_chunks:
            mc = jnp.max(s, axis=0, keepdims=True)
            m_new = mc if c == 0 else jnp.maximum(m, mc)
            p = jnp.exp2(s - m_new)
            ps = jnp.sum(p, axis=0, keepdims=True)
            alpha = None if c == 0 else jnp.exp2(m - m_new)
            l = ps if c == 0 else alpha * l + ps
            pending = (c, p.astype(jnp.bfloat16), alpha)
            m = m_new
    o_ref[0] = (acc / l).T.astype(o_ref.dtype)


def _mla_attention(qa, ka, vat):
    B, S, _ = qa.shape
    tq = TQ_MLA
    return pl.pallas_call(
        _mla_kernel,
        grid=(B, H_A, S // tq),
        in_specs=[pl.BlockSpec((1, tq, QK_PAD), lambda b, h, i: (b, i, h)),
                  pl.BlockSpec((1, S, QK_PAD), lambda b, h, i: (b, 0, h)),
                  pl.BlockSpec((V_DIM, S), lambda b, h, i: (h, b))],
        out_specs=pl.BlockSpec((1, tq, V_DIM), lambda b, h, i: (b, i, h)),
        out_shape=jax.ShapeDtypeStruct((B, S, H_A * V_DIM), MIX_DTYPE),
        compiler_params=_params(("parallel", "parallel", "arbitrary")),
        name="mla_attention",
    )(qa, ka, vat)


def _window_kernel(q_ref, kp_ref, kc_ref, kn_ref, vp_ref, vc_ref, vn_ref, bias_ref, sink_ref, o_ref, ot_ref,
                   *, n_steps):
    step = pl.program_id(1)
    gw = GROUP * HD_B
    k_all = jnp.concatenate([kp_ref[0], kc_ref[0], kn_ref[0]], axis=0)
    vt_all = jnp.concatenate([vp_ref[...], vc_ref[...], vn_ref[...]], axis=1)
    lane = lax.broadcasted_iota(jnp.int32, (Q_BLOCK, gw), 1)
    work = [(i, g) for i in range(NQ_WIN) for g in range(KV_B)]

    def scores(t):
        i, g = work[t]
        qg = q_ref[0, i * Q_BLOCK:(i + 1) * Q_BLOCK, g * gw:(g + 1) * gw]
        wt = jnp.concatenate(
            [jnp.where((lane >= j * HD_B) & (lane < (j + 1) * HD_B), qg, jnp.zeros_like(qg))
             for j in range(GROUP)], axis=0)
        kg = k_all[i * Q_BLOCK:i * Q_BLOCK + SPAN, g * gw:(g + 1) * gw]
        return lax.dot_general(kg, wt, (((1,), (1,)), ((), ())), preferred_element_type=jnp.float32)

    def softmax(t, s):
        i, g = work[t]
        parts = []
        for part in range(3):
            sp = s[part * Q_BLOCK:(part + 1) * Q_BLOCK] + bias_ref[g, part]
            if part == 0 and i == 0:
                sp = jnp.where(step > 0, sp, NEG_INF)
            if part == 2 and i == NQ_WIN - 1:
                sp = jnp.where(step < n_steps - 1, sp, NEG_INF)
            parts.append(sp)
        s = jnp.concatenate(parts, axis=0)
        sink = sink_ref[g]
        m = jnp.maximum(jnp.max(s, axis=0, keepdims=True), sink)
        p = jnp.exp2(s - m)
        l = jnp.sum(p, axis=0, keepdims=True) + jnp.exp2(sink - m)
        return p.astype(jnp.bfloat16), l

    def values(t, p, l):
        i, g = work[t]
        vt = vt_all[g * HD_B:(g + 1) * HD_B, i * Q_BLOCK:i * Q_BLOCK + SPAN]
        r = jnp.dot(vt, p, preferred_element_type=jnp.float32) / l
        for j in range(GROUP):
            hd = g * GROUP + j
            ot_ref[hd * HD_B:(hd + 1) * HD_B, i * Q_BLOCK:(i + 1) * Q_BLOCK] = r[:, j * Q_BLOCK:(j + 1) * Q_BLOCK]

    s_next = scores(0)
    pending = None
    for t in range(len(work) + 1):
        if t < len(work):
            s = s_next
            if t + 1 < len(work):
                s_next = scores(t + 1)
        if pending is not None:
            values(*pending)
        if t < len(work):
            pending = (t,) + softmax(t, s)
    o_ref[0] = ot_ref[...].T.astype(o_ref.dtype)


def _window_attention(qb, kb_rep, vbt, bias, sink_rows):
    B, S, _ = qb.shape
    nblk = S // Q_BLOCK
    rows = NQ_WIN * Q_BLOCK
    n_steps = S // rows
    w = H_B * HD_B
    kprev = lambda b, s: (b, jnp.maximum(s * NQ_WIN - 1, 0), 0)
    cur = lambda b, s: (b, s, 0)
    knext = lambda b, s: (b, jnp.minimum((s + 1) * NQ_WIN, nblk - 1), 0)
    vprev = lambda b, s: (0, b * nblk + jnp.maximum(s * NQ_WIN - 1, 0))
    vcur = lambda b, s: (0, b * n_steps + s)
    vnext = lambda b, s: (0, b * nblk + jnp.minimum((s + 1) * NQ_WIN, nblk - 1))
    kvw = KV_B * HD_B
    return pl.pallas_call(
        functools.partial(_window_kernel, n_steps=n_steps),
        grid=(B, n_steps),
        in_specs=[pl.BlockSpec((1, rows, w), cur),
                  pl.BlockSpec((1, Q_BLOCK, w), kprev),
                  pl.BlockSpec((1, rows, w), cur),
                  pl.BlockSpec((1, Q_BLOCK, w), knext),
                  pl.BlockSpec((kvw, Q_BLOCK), vprev),
                  pl.BlockSpec((kvw, rows), vcur),
                  pl.BlockSpec((kvw, Q_BLOCK), vnext),
                  _const_spec(bias.shape),
                  _const_spec(sink_rows.shape)],
        out_specs=pl.BlockSpec((1, rows, w), cur),
        out_shape=jax.ShapeDtypeStruct((B, S, w), MIX_DTYPE),
        scratch_shapes=[pltpu.VMEM((w, rows), jnp.float32)],
        compiler_params=_params(("parallel", "arbitrary")),
        name="window_attention",
    )(qb, kb_rep, kb_rep, kb_rep, vbt, vbt, vbt, bias, sink_rows)


def _mix_ffn_kernel(xp_ref, xc_ref, xn_ref, ap_ref, ac_ref, an_ref, bp_ref, bc_ref, bn_ref,
                    gp_ref, gc_ref, gn_ref, wout_ref, g2_ref, wup_ref, cw_ref, cb_ref, wdn_ref, gf_ref,
                    o_ref, act_ref, *, n_tiles, final_norm):
    i = pl.program_id(1)
    tm = xc_ref.shape[1]
    f32 = jnp.float32

    def ext(p_ref, c_ref, n_ref):
        return jnp.concatenate([p_ref[0], c_ref[0], n_ref[0]], axis=0)

    gates = ext(gp_ref, gc_ref, gn_ref)
    mixed = (jax.nn.sigmoid(gates[:, 0:D_MODEL].astype(f32)) * ext(ap_ref, ac_ref, an_ref).astype(f32)
             + jax.nn.sigmoid(gates[:, D_MODEL:2 * D_MODEL].astype(f32)) * ext(bp_ref, bc_ref, bn_ref).astype(f32))
    x1e = ext(xp_ref, xc_ref, xn_ref) + jnp.dot(mixed.astype(jnp.bfloat16), wout_ref[...],
                                                preferred_element_type=f32)
    x1 = x1e[HALO:HALO + tm]
    h2e = _rms(x1e, g2_ref[...]).astype(jnp.bfloat16)
    hp = jnp.where(i > 0, h2e[0:HALO], jnp.zeros((HALO, D_MODEL), jnp.bfloat16))
    hn = jnp.where(i < n_tiles - 1, h2e[HALO + tm:], jnp.zeros((HALO, D_MODEL), jnp.bfloat16))
    he = jnp.concatenate([hp, h2e[HALO:HALO + tm], hn], axis=0)
    n_chunks = D_FF // FF_CHUNK

    def cols(ref, j):
        g0 = j * FF_CHUNK
        return jnp.concatenate([ref[:, g0:g0 + FF_CHUNK], ref[:, D_FF + g0:D_FF + g0 + FF_CHUNK]], axis=1)

    def up(j):
        return jnp.dot(he, cols(wup_ref, j), preferred_element_type=jnp.float32)

    u_next = up(0)
    for j in range(n_chunks):
        u = u_next
        if j + 1 < n_chunks:
            u_next = up(j + 1)
        cw = cols(cw_ref, j)
        cb = cols(cb_ref, j)
        y = (u[HALO - 1:HALO - 1 + tm] * cw[0:1] + u[HALO:HALO + tm] * cw[1:2]
             + u[HALO + 1:HALO + 1 + tm] * cw[2:3] + cb)
        gate = y[:, 0:FF_CHUNK]
        val = y[:, FF_CHUNK:2 * FF_CHUNK]
        act_ref[:, j * FF_CHUNK:(j + 1) * FF_CHUNK] = (jax.nn.silu(gate) * val).astype(jnp.bfloat16)
    x2 = x1 + jnp.dot(act_ref[...], wdn_ref[...], preferred_element_type=jnp.float32)
    o_ref[0] = _rms(x2, gf_ref[...]) if final_norm else x2


def _mix_ffn(x, oa, ob, gates, w_out, norm2_g, wup, cw, cb, wdn, final_g, final_norm):
    B, S, _ = x.shape
    tm = TM_FFN
    n_tiles = S // tm
    hb = tm // HALO
    cur = lambda b, i: (b, i, 0)
    prev = lambda b, i: (b, jnp.maximum(i * hb - 1, 0), 0)
    nxt = lambda b, i: (b, jnp.minimum((i + 1) * hb, S // HALO - 1), 0)

    def halo_specs(width):
        return [pl.BlockSpec((1, HALO, width), prev), pl.BlockSpec((1, tm, width), cur),
                pl.BlockSpec((1, HALO, width), nxt)]

    return pl.pallas_call(
        functools.partial(_mix_ffn_kernel, n_tiles=n_tiles, final_norm=final_norm),
        grid=(B, n_tiles),
        in_specs=(halo_specs(D_MODEL) + halo_specs(D_MODEL) + halo_specs(D_MODEL) + halo_specs(2 * D_MODEL)
                  + [_const_spec(w_out.shape), _const_spec((1, D_MODEL)), _const_spec(wup.shape),
                     _const_spec(cw.shape), _const_spec(cb.shape), _const_spec(wdn.shape),
                     _const_spec((1, D_MODEL))]),
        out_specs=pl.BlockSpec((1, tm, D_MODEL), cur),
        out_shape=jax.ShapeDtypeStruct((B, S, D_MODEL), jnp.float32),
        scratch_shapes=[pltpu.VMEM((tm, D_FF), jnp.bfloat16)],
        compiler_params=_params(("parallel", "arbitrary")),
        name="mix_ffn",
    )(x, x, x, oa, oa, oa, ob, ob, ob, gates, gates, gates, w_out, norm2_g, wup, cw, cb, wdn, final_g)


def _prep_w_in(w):
    pad = jnp.zeros((D_MODEL, LANES - QK_ROPE), w.dtype)
    split = Q_LORA + KV_LORA + QK_ROPE
    return jnp.concatenate([w[:, :split], pad, w[:, split:]], axis=1).astype(jnp.bfloat16)


def _prep_w_q_b(w):
    w3 = w.reshape(Q_LORA, H_A, QK_NOPE + QK_ROPE)
    pad = jnp.zeros((Q_LORA, H_A, QK_PAD - QK_NOPE - QK_ROPE), w.dtype)
    return jnp.concatenate([w3, pad], axis=-1).reshape(Q_LORA, H_A * QK_PAD).astype(jnp.bfloat16)


def _prep_w_kv_b(w):
    w3 = w.reshape(KV_LORA, H_A, QK_NOPE + V_DIM)
    k = w3[:, :, :QK_NOPE].reshape(KV_LORA, H_A * QK_NOPE)
    v = w3[:, :, QK_NOPE:].reshape(KV_LORA, H_A * V_DIM)
    return jnp.concatenate([k, v], axis=1).astype(jnp.bfloat16)


def kernel(x, positions, norm1_g, w_in, q_a_norm_g, w_q_b, kv_a_norm_g, w_kv_b, rel_bias, sinks,
           w_out, norm2_g, w_up, conv_w, conv_b, w_down, final_norm_g):
    B, S, D = x.shape
    depth = norm1_g.shape[0]
    cos, sin = _rope_tables(positions)
    bias = _window_bias(rel_bias)
    f32 = jnp.float32
    for l in range(depth):
        x2d = x.reshape(B * S, D)
        qa, ka, vat, qb, kb_rep, vbt, gates = _input_projection(
            x2d, norm1_g[l][None].astype(f32), _prep_w_in(w_in[l]),
            q_a_norm_g[l][None].astype(f32), _prep_w_q_b(w_q_b[l]),
            kv_a_norm_g[l][None].astype(f32), _prep_w_kv_b(w_kv_b[l]), cos, sin, S)
        r3 = lambda a: a.reshape(B, S, a.shape[-1])
        oa = _mla_attention(r3(qa), r3(ka), vat)
        sink_rows = jnp.repeat(sinks[l].astype(f32) * LOG2E, Q_BLOCK).reshape(KV_B, 1, GROUP * Q_BLOCK)
        ob = _window_attention(r3(qb), r3(kb_rep), vbt, bias, sink_rows)
        x = _mix_ffn(x, oa, ob, r3(gates), w_out[l].astype(jnp.bfloat16), norm2_g[l][None].astype(f32),
                     w_up[l].astype(jnp.bfloat16), conv_w[l].astype(f32), conv_b[l][None].astype(f32),
                     w_down[l].astype(jnp.bfloat16), final_norm_g[None].astype(f32),
                     final_norm=(l == depth - 1))
    return x
```

```python
import functools
import math

import jax
import jax.numpy as jnp
import numpy as np
from jax import lax
from jax.experimental import pallas as pl
from jax.experimental.pallas import tpu as pltpu

D_MODEL = 1024
EPS = 1e-6
Q_BLOCK = 128
H_A = 8
QK_NOPE = 128
QK_ROPE = 64
V_DIM = 128
Q_LORA = 256
KV_LORA = 128
ROPE_THETA = 10000.0
H_B = 16
KV_B = 4
GROUP = H_B // KV_B
HD_B = 64
WINDOW = 128
NUM_BUCKETS = 32
MAX_DISTANCE = 128
D_FF = 2816
SPAN = Q_BLOCK + 2 * WINDOW

LANES = 128
QK_PAD = 256
NEG_INF = -1e30
LOG2E = math.log2(math.e)
MIX_DTYPE = jnp.bfloat16
VMEM_LIMIT = 56 * 1024 * 1024

TM_PROJ = 512
TQ_MLA = 512
TK_MLA = 1024
MLA_LOOKAHEAD = 1
MLA_HEADS = 1
MLA_SAFE_LOGIT = 64.0
NQ_WIN = 4
TM_FFN = 512
FF_CHUNK = 256
HALO = 16


def _const_spec(shape):
    nd = len(shape)
    return pl.BlockSpec(shape, lambda *_: (0,) * nd, pipeline_mode=pl.Buffered(1))


def _params(sem):
    return pltpu.CompilerParams(dimension_semantics=sem, vmem_limit_bytes=VMEM_LIMIT)


def _rms(x, g):
    return x * lax.rsqrt(jnp.mean(x * x, axis=-1, keepdims=True) + EPS) * g


def _swap_halves_32(x):
    lane = lax.broadcasted_iota(jnp.int32, x.shape, 1)
    up = pltpu.roll(x, LANES - 32, 1)
    down = pltpu.roll(x, 32, 1)
    return jnp.where((lane & 32) == 0, up, down)


def _rope_table_kernel(pos_ref, inv_ref, sign_ref, cos_ref, sin_ref):
    ang = pos_ref[...] * inv_ref[...]
    cos_ref[...] = jnp.cos(ang)
    sin_ref[...] = jnp.sin(ang) * sign_ref[...]


def _rope_tables(positions):
    S = positions.shape[0]
    half = QK_ROPE // 2
    inv_freq = ROPE_THETA ** (-jnp.arange(half, dtype=jnp.float32) / half)
    inv = jnp.tile(inv_freq, LANES // half)[None, :]
    sign = jnp.tile(jnp.concatenate([-jnp.ones(half, jnp.float32), jnp.ones(half, jnp.float32)]),
                    LANES // QK_ROPE)[None, :]
    pos = positions.astype(jnp.float32)[:, None]
    ts = 512
    return pl.pallas_call(
        _rope_table_kernel,
        grid=(S // ts,),
        in_specs=[pl.BlockSpec((ts, 1), lambda i: (i, 0)),
                  pl.BlockSpec((1, LANES), lambda i: (0, 0)),
                  pl.BlockSpec((1, LANES), lambda i: (0, 0))],
        out_specs=[pl.BlockSpec((ts, LANES), lambda i: (i, 0)),
                   pl.BlockSpec((ts, LANES), lambda i: (i, 0))],
        out_shape=[jax.ShapeDtypeStruct((S, LANES), jnp.float32)] * 2,
        compiler_params=_params(("parallel",)),
        name="rope_tables",
    )(pos, inv, sign)


def _t5_bucket(rel):
    nb = NUM_BUCKETS // 2
    max_exact = nb // 2
    base = (rel > 0).astype(jnp.int32) * nb
    n = jnp.abs(rel)
    nf = jnp.maximum(n, 1).astype(jnp.float32)
    large = max_exact + (jnp.log(nf / max_exact) / math.log(MAX_DISTANCE / max_exact)
                         * (nb - max_exact)).astype(jnp.int32)
    large = jnp.minimum(large, nb - 1)
    return base + jnp.where(n < max_exact, n, large)


def _bias_kernel(rel_bias_ref, bucket_ref, band_ref, out_ref):
    g = pl.program_id(0)
    bucket = bucket_ref[0]
    band = band_ref[0]
    for j in range(GROUP):
        acc = jnp.zeros(bucket.shape, jnp.float32)
        for b in range(NUM_BUCKETS):
            acc = jnp.where(bucket == b, rel_bias_ref[b, g * GROUP + j], acc)
        out_ref[0, 0, :, j * Q_BLOCK:(j + 1) * Q_BLOCK] = jnp.where(band != 0, acc * LOG2E, NEG_INF)


def _window_bias(rel_bias):
    a = jnp.arange(Q_BLOCK, dtype=jnp.int32)[None, :]
    c = jnp.arange(SPAN, dtype=jnp.int32)[:, None]
    rel = c - WINDOW - a
    bucket = _t5_bucket(rel).reshape(3, Q_BLOCK, Q_BLOCK)
    band = (jnp.abs(rel) <= WINDOW).astype(jnp.int32).reshape(3, Q_BLOCK, Q_BLOCK)
    part = pl.BlockSpec((1, Q_BLOCK, Q_BLOCK), lambda g, p: (p, 0, 0))
    return pl.pallas_call(
        _bias_kernel,
        grid=(KV_B, 3),
        in_specs=[pl.BlockSpec(memory_space=pltpu.SMEM), part, part],
        out_specs=pl.BlockSpec((1, 1, Q_BLOCK, GROUP * Q_BLOCK), lambda g, p: (g, p, 0, 0)),
        out_shape=jax.ShapeDtypeStruct((KV_B, 3, Q_BLOCK, GROUP * Q_BLOCK), jnp.float32),
        compiler_params=_params(("arbitrary", "arbitrary")),
        name="window_bias",
    )(rel_bias.astype(jnp.float32), bucket, band)


def _proj_kernel(x_ref, g1_ref, w_in_ref, gq_ref, wq_ref, gkv_ref, wkv_ref, cos_ref, sin_ref,
                 qa_ref, ka_ref, vat_ref, qb_ref, kb_ref, vbt_ref, gate_ref, *, scale_a, scale_b):
    x = x_ref[...]
    h = _rms(x, g1_ref[...]).astype(jnp.bfloat16)
    cos = cos_ref[...]
    sin = sin_ref[...]

    def rope(blk):
        return blk * cos + _swap_halves_32(blk) * sin

    lat = jnp.dot(h, w_in_ref[:, 0:512], preferred_element_type=jnp.float32)
    q_lat = lat[:, 0:Q_LORA]
    c_kv = lat[:, Q_LORA:Q_LORA + KV_LORA]
    k_rope = rope(lat[:, 384:512]).astype(jnp.bfloat16)

    qn = _rms(q_lat, gq_ref[...]).astype(jnp.bfloat16)
    q = jnp.dot(qn, wq_ref[...], preferred_element_type=jnp.float32)
    for hd in range(H_A):
        base = hd * QK_PAD
        qa_ref[:, base:base + QK_NOPE] = (q[:, base:base + QK_NOPE] * scale_a).astype(jnp.bfloat16)
        qa_ref[:, base + QK_NOPE:base + QK_PAD] = (
            rope(q[:, base + QK_NOPE:base + QK_PAD]) * scale_a).astype(jnp.bfloat16)

    cn = _rms(c_kv, gkv_ref[...]).astype(jnp.bfloat16)
    kv = jnp.dot(cn, wkv_ref[...], preferred_element_type=jnp.float32)
    for hd in range(H_A):
        base = hd * QK_PAD
        ka_ref[:, base:base + QK_NOPE] = kv[:, hd * QK_NOPE:(hd + 1) * QK_NOPE].astype(jnp.bfloat16)
        ka_ref[:, base + QK_NOPE:base + QK_PAD] = k_rope
    vat_ref[...] = kv[:, H_A * QK_NOPE:].T.astype(jnp.bfloat16)

    qkv_b = jnp.dot(h, w_in_ref[:, 512:2048], preferred_element_type=jnp.float32)
    qb_ref[...] = (qkv_b[:, 0:1024] * scale_b).astype(jnp.bfloat16)
    lane = lax.broadcasted_iota(jnp.int32, (x.shape[0], LANES), 1)
    for g in range(KV_B):
        pair = qkv_b[:, 1024 + (g // 2) * LANES:1024 + (g // 2 + 1) * LANES]
        other = pltpu.roll(pair, HD_B, 1)
        first = (lane < HD_B) if g % 2 == 0 else (lane >= HD_B)
        rep = jnp.where(first, pair, other).astype(jnp.bfloat16)
        kb_ref[:, g * 2 * LANES:(g * 2 + 1) * LANES] = rep
        kb_ref[:, (g * 2 + 1) * LANES:(g * 2 + 2) * LANES] = rep
    vbt_ref[...] = qkv_b[:, 1280:1536].T.astype(jnp.bfloat16)

    gate_ref[...] = jnp.dot(h, w_in_ref[:, 2048:4096],
                            preferred_element_type=jnp.float32).astype(gate_ref.dtype)


def _input_projection(x2d, norm1_g, w_in_p, gq, wq_p, gkv, wkv_p, cos, sin, S):
    N = x2d.shape[0]
    tm = TM_PROJ
    s_tiles = S // tm
    row = lambda i: (i, 0)
    tab = lambda i: (i % s_tiles, 0)
    bf = jnp.bfloat16
    col = lambda i: (0, i)
    kvw = KV_B * HD_B
    outs = [((N, H_A * QK_PAD), (tm, H_A * QK_PAD), row, bf),
            ((N, H_A * QK_PAD), (tm, H_A * QK_PAD), row, bf),
            ((H_A * V_DIM, N), (H_A * V_DIM, tm), col, bf),
            ((N, H_B * HD_B), (tm, H_B * HD_B), row, bf),
            ((N, GROUP * kvw), (tm, GROUP * kvw), row, bf),
            ((kvw, N), (kvw, tm), col, bf),
            ((N, 2 * D_MODEL), (tm, 2 * D_MODEL), row, MIX_DTYPE)]
    return pl.pallas_call(
        functools.partial(_proj_kernel, scale_a=LOG2E / math.sqrt(QK_NOPE + QK_ROPE),
                          scale_b=LOG2E / math.sqrt(HD_B)),
        grid=(N // tm,),
        in_specs=[pl.BlockSpec((tm, D_MODEL), row),
                  _const_spec((1, D_MODEL)),
                  _const_spec(w_in_p.shape),
                  _const_spec((1, Q_LORA)),
                  _const_spec(wq_p.shape),
                  _const_spec((1, KV_LORA)),
                  _const_spec(wkv_p.shape),
                  pl.BlockSpec((tm, LANES), tab),
                  pl.BlockSpec((tm, LANES), tab)],
        out_specs=[pl.BlockSpec(blk, im) for _, blk, im, _ in outs],
        out_shape=[jax.ShapeDtypeStruct(shp, dt) for shp, _, _, dt in outs],
        compiler_params=_params(("parallel",)),
        name="input_projection",
    )(x2d, norm1_g, w_in_p, gq, wq_p, gkv, wkv_p, cos, sin)


def _mla_kernel(q_ref, k_ref, vt_ref, o_ref, kmax_ref):
    f32 = jnp.float32
    n_chunks = k_ref.shape[1] // TK_MLA
    heads = range(MLA_HEADS)
    q = [q_ref[0, :, h * QK_PAD:(h + 1) * QK_PAD] for h in heads]

    @pl.when(pl.program_id(2) == 0)
    def _():
        for h in heads:
            kf = k_ref[0, :, h * QK_PAD:(h + 1) * QK_PAD].astype(f32)
            kmax_ref[h] = jnp.max(jnp.sum(kf * kf, axis=1, keepdims=True))

    bound2 = None
    for h in heads:
        qf = q[h].astype(f32)
        b2 = jnp.max(jnp.sum(qf * qf, axis=1, keepdims=True)) * kmax_ref[h]
        bound2 = b2 if bound2 is None else jnp.maximum(bound2, b2)
    safe = bound2 <= MLA_SAFE_LOGIT * MLA_SAFE_LOGIT

    def scores(h, c):
        kc = k_ref[0, c * TK_MLA:(c + 1) * TK_MLA, h * QK_PAD:(h + 1) * QK_PAD]
        return lax.dot_general(kc, q[h], (((1,), (1,)), ((), ())), preferred_element_type=f32)

    def pipeline(softmax, values):
        ahead = [[scores(h, c) for c in range(min(MLA_LOOKAHEAD, n_chunks))] for h in heads]
        pending = [None] * MLA_HEADS
        s = [None] * MLA_HEADS
        for c in range(n_chunks + 1):
            for h in heads:
                if c < n_chunks:
                    s[h] = ahead[h].pop(0)
                    if c + MLA_LOOKAHEAD < n_chunks:
                        ahead[h].append(scores(h, c + MLA_LOOKAHEAD))
            for h in heads:
                if pending[h] is not None:
                    values(h, c - 1, *pending[h])
                    pending[h] = None
            for h in heads:
                if c < n_chunks:
                    pending[h] = softmax(h, c, s[h])

    @pl.when(safe)
    def _():
        l = [None] * MLA_HEADS
        acc = [None] * MLA_HEADS

        def softmax(h, c, s):
            p = jnp.exp2(s)
            ps = jnp.sum(p, axis=0, keepdims=True)
            l[h] = ps if c == 0 else l[h] + ps
            return (p.astype(jnp.bfloat16),)

        def values(h, c, p):
            pv = jnp.dot(vt_ref[h * V_DIM:(h + 1) * V_DIM, c * TK_MLA:(c + 1) * TK_MLA], p,
                         preferred_element_type=f32)
            acc[h] = pv if acc[h] is None else acc[h] + pv

        pipeline(softmax, values)
        for h in heads:
            o_ref[0, :, h * V_DIM:(h + 1) * V_DIM] = (acc[h] / l[h]).T.astype(o_ref.dtype)

    @pl.when(jnp.logical_not(safe))
    def _():
        m = [None] * MLA_HEADS
        l = [None] * MLA_HEADS
        acc = [None] * MLA_HEADS

        def softmax(h, c, s):
            mc = jnp.max(s, axis=0, keepdims=True)
            m_new = mc if c == 0 else jnp.maximum(m[h], mc)
            p = jnp.exp2(s - m_new)
            ps = jnp.sum(p, axis=0, keepdims=True)
            alpha = None if c == 0 else jnp.exp2(m[h] - m_new)
            l[h] = ps if c == 0 else alpha * l[h] + ps
            m[h] = m_new
            return p.astype(jnp.bfloat16), alpha

        def values(h, c, p, alpha):
            pv = jnp.dot(vt_ref[h * V_DIM:(h + 1) * V_DIM, c * TK_MLA:(c + 1) * TK_MLA], p,
                         preferred_element_type=f32)
            acc[h] = pv if alpha is None else alpha * acc[h] + pv

        pipeline(softmax, values)
        for h in heads:
            o_ref[0, :, h * V_DIM:(h + 1) * V_DIM] = (acc[h] / l[h]).T.astype(o_ref.dtype)


def _mla_attention(qa, ka, vat):
    B, S, _ = qa.shape
    tq = TQ_MLA
    nh = MLA_HEADS
    return pl.pallas_call(
        _mla_kernel,
        grid=(B, H_A // nh, S // tq),
        in_specs=[pl.BlockSpec((1, tq, nh * QK_PAD), lambda b, h, i: (b, i, h)),
                  pl.BlockSpec((1, S, nh * QK_PAD), lambda b, h, i: (b, 0, h)),
                  pl.BlockSpec((nh * V_DIM, S), lambda b, h, i: (h, b))],
        out_specs=pl.BlockSpec((1, tq, nh * V_DIM), lambda b, h, i: (b, i, h)),
        out_shape=jax.ShapeDtypeStruct((B, S, H_A * V_DIM), MIX_DTYPE),
        scratch_shapes=[pltpu.SMEM((nh,), jnp.float32)],
        compiler_params=_params(("parallel", "parallel", "arbitrary")),
        name="mla_attention",
    )(qa, ka, vat)


def _window_kernel(q_ref, kp_ref, kc_ref, kn_ref, vp_ref, vc_ref, vn_ref, bias_ref, sink_ref, o_ref, ot_ref,
                   *, n_steps):
    step = pl.program_id(1)
    gw = GROUP * HD_B
    k_all = jnp.concatenate([kp_ref[0], kc_ref[0], kn_ref[0]], axis=0)
    vt_all = jnp.concatenate([vp_ref[...], vc_ref[...], vn_ref[...]], axis=1)
    lane = lax.broadcasted_iota(jnp.int32, (Q_BLOCK, gw), 1)
    work = [(i, g) for i in range(NQ_WIN) for g in range(KV_B)]

    def scores(t):
        i, g = work[t]
        qg = q_ref[0, i * Q_BLOCK:(i + 1) * Q_BLOCK, g * gw:(g + 1) * gw]
        wt = jnp.concatenate(
            [jnp.where((lane >= j * HD_B) & (lane < (j + 1) * HD_B), qg, jnp.zeros_like(qg))
             for j in range(GROUP)], axis=0)
        kg = k_all[i * Q_BLOCK:i * Q_BLOCK + SPAN, g * gw:(g + 1) * gw]
        return lax.dot_general(kg, wt, (((1,), (1,)), ((), ())), preferred_element_type=jnp.float32)

    def softmax(t, s):
        i, g = work[t]
        parts = []
        for part in range(3):
            sp = s[part * Q_BLOCK:(part + 1) * Q_BLOCK] + bias_ref[g, part]
            if part == 0 and i == 0:
                sp = jnp.where(step > 0, sp, NEG_INF)
            if part == 2 and i == NQ_WIN - 1:
                sp = jnp.where(step < n_steps - 1, sp, NEG_INF)
            parts.append(sp)
        s = jnp.concatenate(parts, axis=0)
        sink = sink_ref[g]
        m = jnp.maximum(jnp.max(s, axis=0, keepdims=True), sink)
        p = jnp.exp2(s - m)
        l = jnp.sum(p, axis=0, keepdims=True) + jnp.exp2(sink - m)
        return p.astype(jnp.bfloat16), l

    def values(t, p, l):
        i, g = work[t]
        vt = vt_all[g * HD_B:(g + 1) * HD_B, i * Q_BLOCK:i * Q_BLOCK + SPAN]
        r = jnp.dot(vt, p, preferred_element_type=jnp.float32) / l
        for j in range(GROUP):
            hd = g * GROUP + j
            ot_ref[hd * HD_B:(hd + 1) * HD_B, i * Q_BLOCK:(i + 1) * Q_BLOCK] = r[:, j * Q_BLOCK:(j + 1) * Q_BLOCK]

    s_next = scores(0)
    pending = None
    for t in range(len(work) + 1):
        if t < len(work):
            s = s_next
            if t + 1 < len(work):
                s_next = scores(t + 1)
        if pending is not None:
            values(*pending)
        if t < len(work):
            pending = (t,) + softmax(t, s)
    o_ref[0] = ot_ref[...].T.astype(o_ref.dtype)


def _window_attention(qb, kb_rep, vbt, bias, sink_rows):
    B, S, _ = qb.shape
    nblk = S // Q_BLOCK
    rows = NQ_WIN * Q_BLOCK
    n_steps = S // rows
    w = H_B * HD_B
    kprev = lambda b, s: (b, jnp.maximum(s * NQ_WIN - 1, 0), 0)
    cur = lambda b, s: (b, s, 0)
    knext = lambda b, s: (b, jnp.minimum((s + 1) * NQ_WIN, nblk - 1), 0)
    vprev = lambda b, s: (0, b * nblk + jnp.maximum(s * NQ_WIN - 1, 0))
    vcur = lambda b, s: (0, b * n_steps + s)
    vnext = lambda b, s: (0, b * nblk + jnp.minimum((s + 1) * NQ_WIN, nblk - 1))
    kvw = KV_B * HD_B
    return pl.pallas_call(
        functools.partial(_window_kernel, n_steps=n_steps),
        grid=(B, n_steps),
        in_specs=[pl.BlockSpec((1, rows, w), cur),
                  pl.BlockSpec((1, Q_BLOCK, w), kprev),
                  pl.BlockSpec((1, rows, w), cur),
                  pl.BlockSpec((1, Q_BLOCK, w), knext),
                  pl.BlockSpec((kvw, Q_BLOCK), vprev),
                  pl.BlockSpec((kvw, rows), vcur),
                  pl.BlockSpec((kvw, Q_BLOCK), vnext),
                  _const_spec(bias.shape),
                  _const_spec(sink_rows.shape)],
        out_specs=pl.BlockSpec((1, rows, w), cur),
        out_shape=jax.ShapeDtypeStruct((B, S, w), MIX_DTYPE),
        scratch_shapes=[pltpu.VMEM((w, rows), jnp.float32)],
        compiler_params=_params(("parallel", "arbitrary")),
        name="window_attention",
    )(qb, kb_rep, kb_rep, kb_rep, vbt, vbt, vbt, bias, sink_rows)


def _mix_ffn_kernel(xp_ref, xc_ref, xn_ref, ap_ref, ac_ref, an_ref, bp_ref, bc_ref, bn_ref,
                    gp_ref, gc_ref, gn_ref, wout_ref, g2_ref, wup_ref, cw_ref, cb_ref, wdn_ref, gf_ref,
                    o_ref, act_ref, *, n_tiles, final_norm):
    i = pl.program_id(1)
    tm = xc_ref.shape[1]
    f32 = jnp.float32

    def ext(p_ref, c_ref, n_ref):
        return jnp.concatenate([p_ref[0], c_ref[0], n_ref[0]], axis=0)

    gates = ext(gp_ref, gc_ref, gn_ref)
    mixed = (jax.nn.sigmoid(gates[:, 0:D_MODEL].astype(f32)) * ext(ap_ref, ac_ref, an_ref).astype(f32)
             + jax.nn.sigmoid(gates[:, D_MODEL:2 * D_MODEL].astype(f32)) * ext(bp_ref, bc_ref, bn_ref).astype(f32))
    x1e = ext(xp_ref, xc_ref, xn_ref) + jnp.dot(mixed.astype(jnp.bfloat16), wout_ref[...],
                                                preferred_element_type=f32)
    x1 = x1e[HALO:HALO + tm]
    h2e = _rms(x1e, g2_ref[...]).astype(jnp.bfloat16)
    hp = jnp.where(i > 0, h2e[0:HALO], jnp.zeros((HALO, D_MODEL), jnp.bfloat16))
    hn = jnp.where(i < n_tiles - 1, h2e[HALO + tm:], jnp.zeros((HALO, D_MODEL), jnp.bfloat16))
    he = jnp.concatenate([hp, h2e[HALO:HALO + tm], hn], axis=0)
    n_chunks = D_FF // FF_CHUNK

    def cols(ref, j):
        g0 = j * FF_CHUNK
        return jnp.concatenate([ref[:, g0:g0 + FF_CHUNK], ref[:, D_FF + g0:D_FF + g0 + FF_CHUNK]], axis=1)

    def up(j):
        return jnp.dot(he, cols(wup_ref, j), preferred_element_type=jnp.float32)

    u_next = up(0)
    for j in range(n_chunks):
        u = u_next
        if j + 1 < n_chunks:
            u_next = up(j + 1)
        cw = cols(cw_ref, j)
        cb = cols(cb_ref, j)
        y = (u[HALO - 1:HALO - 1 + tm] * cw[0:1] + u[HALO:HALO + tm] * cw[1:2]
             + u[HALO + 1:HALO + 1 + tm] * cw[2:3] + cb)
        gate = y[:, 0:FF_CHUNK]
        val = y[:, FF_CHUNK:2 * FF_CHUNK]
        act_ref[:, j * FF_CHUNK:(j + 1) * FF_CHUNK] = (jax.nn.silu(gate) * val).astype(jnp.bfloat16)
    x2 = x1 + jnp.dot(act_ref[...], wdn_ref[...], preferred_element_type=jnp.float32)
    o_ref[0] = _rms(x2, gf_ref[...]) if final_norm else x2


def _mix_ffn(x, oa, ob, gates, w_out, norm2_g, wup, cw, cb, wdn, final_g, final_norm):
    B, S, _ = x.shape
    tm = TM_FFN
    n_tiles = S // tm
    hb = tm // HALO
    cur = lambda b, i: (b, i, 0)
    prev = lambda b, i: (b, jnp.maximum(i * hb - 1, 0), 0)
    nxt = lambda b, i: (b, jnp.minimum((i + 1) * hb, S // HALO - 1), 0)

    def halo_specs(width):
        return [pl.BlockSpec((1, HALO, width), prev), pl.BlockSpec((1, tm, width), cur),
                pl.BlockSpec((1, HALO, width), nxt)]

    return pl.pallas_call(
        functools.partial(_mix_ffn_kernel, n_tiles=n_tiles, final_norm=final_norm),
        grid=(B, n_tiles),
        in_specs=(halo_specs(D_MODEL) + halo_specs(D_MODEL) + halo_specs(D_MODEL) + halo_specs(2 * D_MODEL)
                  + [_const_spec(w_out.shape), _const_spec((1, D_MODEL)), _const_spec(wup.shape),
                     _const_spec(cw.shape), _const_spec(cb.shape), _const_spec(wdn.shape),
                     _const_spec((1, D_MODEL))]),
        out_specs=pl.BlockSpec((1, tm, D_MODEL), cur),
        out_shape=jax.ShapeDtypeStruct((B, S, D_MODEL), jnp.float32),
        scratch_shapes=[pltpu.VMEM((tm, D_FF), jnp.bfloat16)],
        compiler_params=_params(("parallel", "arbitrary")),
        name="mix_ffn",
    )(x, x, x, oa, oa, oa, ob, ob, ob, gates, gates, gates, w_out, norm2_g, wup, cw, cb, wdn, final_g)


def _prep_w_in(w):
    pad = jnp.zeros((D_MODEL, LANES - QK_ROPE), w.dtype)
    split = Q_LORA + KV_LORA + QK_ROPE
    return jnp.concatenate([w[:, :split], pad, w[:, split:]], axis=1).astype(jnp.bfloat16)


def _prep_w_q_b(w):
    w3 = w.reshape(Q_LORA, H_A, QK_NOPE + QK_ROPE)
    pad = jnp.zeros((Q_LORA, H_A, QK_PAD - QK_NOPE - QK_ROPE), w.dtype)
    return jnp.concatenate([w3, pad], axis=-1).reshape(Q_LORA, H_A * QK_PAD).astype(jnp.bfloat16)


def _prep_w_kv_b(w):
    w3 = w.reshape(KV_LORA, H_A, QK_NOPE + V_DIM)
    k = w3[:, :, :QK_NOPE].reshape(KV_LORA, H_A * QK_NOPE)
    v = w3[:, :, QK_NOPE:].reshape(KV_LORA, H_A * V_DIM)
    return jnp.concatenate([k, v], axis=1).astype(jnp.bfloat16)


def kernel(x, positions, norm1_g, w_in, q_a_norm_g, w_q_b, kv_a_norm_g, w_kv_b, rel_bias, sinks,
           w_out, norm2_g, w_up, conv_w, conv_b, w_down, final_norm_g):
    B, S, D = x.shape
    depth = norm1_g.shape[0]
    cos, sin = _rope_tables(positions)
    bias = _window_bias(rel_bias)
    f32 = jnp.float32
    for l in range(depth):
        x2d = x.reshape(B * S, D)
        qa, ka, vat, qb, kb_rep, vbt, gates = _input_projection(
            x2d, norm1_g[l][None].astype(f32), _prep_w_in(w_in[l]),
            q_a_norm_g[l][None].astype(f32), _prep_w_q_b(w_q_b[l]),
            kv_a_norm_g[l][None].astype(f32), _prep_w_kv_b(w_kv_b[l]), cos, sin, S)
        r3 = lambda a: a.reshape(B, S, a.shape[-1])
        oa = _mla_attention(r3(qa), r3(ka), vat)
        sink_rows = jnp.repeat(sinks[l].astype(f32) * LOG2E, Q_BLOCK).reshape(KV_B, 1, GROUP * Q_BLOCK)
        ob = _window_attention(r3(qb), r3(kb_rep), vbt, bias, sink_rows)
        x = _mix_ffn(x, oa, ob, r3(gates), w_out[l].astype(jnp.bfloat16), norm2_g[l][None].astype(f32),
                     w_up[l].astype(jnp.bfloat16), conv_w[l].astype(f32), conv_b[l][None].astype(f32),
                     w_down[l].astype(jnp.bfloat16), final_norm_g[None].astype(f32),
                     final_norm=(l == depth - 1))
    return x
```

```python
import functools
import math

import jax
import jax.numpy as jnp
import numpy as np
from jax import lax
from jax.experimental import pallas as pl
from jax.experimental.pallas import tpu as pltpu

D_MODEL = 1024
EPS = 1e-6
Q_BLOCK = 128
H_A = 8
QK_NOPE = 128
QK_ROPE = 64
V_DIM = 128
Q_LORA = 256
KV_LORA = 128
ROPE_THETA = 10000.0
H_B = 16
KV_B = 4
GROUP = H_B // KV_B
HD_B = 64
WINDOW = 128
NUM_BUCKETS = 32
MAX_DISTANCE = 128
D_FF = 2816
SPAN = Q_BLOCK + 2 * WINDOW

LANES = 128
QK_PAD = 256
NEG_INF = -1e30
LOG2E = math.log2(math.e)
MIX_DTYPE = jnp.bfloat16
MIN_DENOM = 2.0 ** -60
VMEM_LIMIT = 56 * 1024 * 1024

TM_PROJ = 512
TQ_MLA = 512
TK_MLA = 1024
MLA_LOOKAHEAD = 1
MLA_HEADS = 1
NQ_WIN = 4
TM_FFN = 512
FF_CHUNK = 256
HALO = 16


def _const_spec(shape):
    nd = len(shape)
    return pl.BlockSpec(shape, lambda *_: (0,) * nd, pipeline_mode=pl.Buffered(1))


def _params(sem):
    return pltpu.CompilerParams(dimension_semantics=sem, vmem_limit_bytes=VMEM_LIMIT)


def _rms(x, g):
    return x * lax.rsqrt(jnp.mean(x * x, axis=-1, keepdims=True) + EPS) * g


def _count_unsafe(o, l):
    bad_o = jnp.where(jnp.isfinite(o), 0.0, 1.0)
    bad_l = jnp.where(jnp.isfinite(l) & (l >= MIN_DENOM), 0.0, 1.0)
    return jnp.sum(bad_o) + jnp.sum(bad_l)


def _swap_halves_32(x):
    lane = lax.broadcasted_iota(jnp.int32, x.shape, 1)
    up = pltpu.roll(x, LANES - 32, 1)
    down = pltpu.roll(x, 32, 1)
    return jnp.where((lane & 32) == 0, up, down)


def _rope_table_kernel(pos_ref, inv_ref, sign_ref, cos_ref, sin_ref):
    ang = pos_ref[...] * inv_ref[...]
    cos_ref[...] = jnp.cos(ang)
    sin_ref[...] = jnp.sin(ang) * sign_ref[...]


def _rope_tables(positions):
    S = positions.shape[0]
    half = QK_ROPE // 2
    inv_freq = ROPE_THETA ** (-jnp.arange(half, dtype=jnp.float32) / half)
    inv = jnp.tile(inv_freq, LANES // half)[None, :]
    sign = jnp.tile(jnp.concatenate([-jnp.ones(half, jnp.float32), jnp.ones(half, jnp.float32)]),
                    LANES // QK_ROPE)[None, :]
    pos = positions.astype(jnp.float32)[:, None]
    ts = 512
    return pl.pallas_call(
        _rope_table_kernel,
        grid=(S // ts,),
        in_specs=[pl.BlockSpec((ts, 1), lambda i: (i, 0)),
                  pl.BlockSpec((1, LANES), lambda i: (0, 0)),
                  pl.BlockSpec((1, LANES), lambda i: (0, 0))],
        out_specs=[pl.BlockSpec((ts, LANES), lambda i: (i, 0)),
                   pl.BlockSpec((ts, LANES), lambda i: (i, 0))],
        out_shape=[jax.ShapeDtypeStruct((S, LANES), jnp.float32)] * 2,
        compiler_params=_params(("parallel",)),
        name="rope_tables",
    )(pos, inv, sign)


def _t5_bucket(rel):
    nb = NUM_BUCKETS // 2
    max_exact = nb // 2
    base = (rel > 0).astype(jnp.int32) * nb
    n = jnp.abs(rel)
    nf = jnp.maximum(n, 1).astype(jnp.float32)
    large = max_exact + (jnp.log(nf / max_exact) / math.log(MAX_DISTANCE / max_exact)
                         * (nb - max_exact)).astype(jnp.int32)
    large = jnp.minimum(large, nb - 1)
    return base + jnp.where(n < max_exact, n, large)


def _bias_kernel(rel_bias_ref, bucket_ref, band_ref, out_ref):
    g = pl.program_id(0)
    bucket = bucket_ref[0]
    band = band_ref[0]
    for j in range(GROUP):
        acc = jnp.zeros(bucket.shape, jnp.float32)
        for b in range(NUM_BUCKETS):
            acc = jnp.where(bucket == b, rel_bias_ref[b, g * GROUP + j], acc)
        out_ref[0, 0, :, j * Q_BLOCK:(j + 1) * Q_BLOCK] = jnp.where(band != 0, acc * LOG2E, NEG_INF)


def _window_bias(rel_bias):
    a = jnp.arange(Q_BLOCK, dtype=jnp.int32)[None, :]
    c = jnp.arange(SPAN, dtype=jnp.int32)[:, None]
    rel = c - WINDOW - a
    bucket = _t5_bucket(rel).reshape(3, Q_BLOCK, Q_BLOCK)
    band = (jnp.abs(rel) <= WINDOW).astype(jnp.int32).reshape(3, Q_BLOCK, Q_BLOCK)
    part = pl.BlockSpec((1, Q_BLOCK, Q_BLOCK), lambda g, p: (p, 0, 0))
    return pl.pallas_call(
        _bias_kernel,
        grid=(KV_B, 3),
        in_specs=[pl.BlockSpec(memory_space=pltpu.SMEM), part, part],
        out_specs=pl.BlockSpec((1, 1, Q_BLOCK, GROUP * Q_BLOCK), lambda g, p: (g, p, 0, 0)),
        out_shape=jax.ShapeDtypeStruct((KV_B, 3, Q_BLOCK, GROUP * Q_BLOCK), jnp.float32),
        compiler_params=_params(("arbitrary", "arbitrary")),
        name="window_bias",
    )(rel_bias.astype(jnp.float32), bucket, band)


def _proj_kernel(x_ref, g1_ref, w_in_ref, gq_ref, wq_ref, gkv_ref, wkv_ref, cos_ref, sin_ref,
                 qa_ref, ka_ref, vat_ref, qb_ref, kb_ref, vbt_ref, gate_ref, *, scale_a, scale_b):
    x = x_ref[...]
    h = _rms(x, g1_ref[...]).astype(jnp.bfloat16)
    cos = cos_ref[...]
    sin = sin_ref[...]

    def rope(blk):
        return blk * cos + _swap_halves_32(blk) * sin

    lat = jnp.dot(h, w_in_ref[:, 0:512], preferred_element_type=jnp.float32)
    q_lat = lat[:, 0:Q_LORA]
    c_kv = lat[:, Q_LORA:Q_LORA + KV_LORA]
    k_rope = rope(lat[:, 384:512]).astype(jnp.bfloat16)

    qn = _rms(q_lat, gq_ref[...]).astype(jnp.bfloat16)
    q = jnp.dot(qn, wq_ref[...], preferred_element_type=jnp.float32)
    for hd in range(H_A):
        base = hd * QK_PAD
        qa_ref[:, base:base + QK_NOPE] = (q[:, base:base + QK_NOPE] * scale_a).astype(jnp.bfloat16)
        qa_ref[:, base + QK_NOPE:base + QK_PAD] = (
            rope(q[:, base + QK_NOPE:base + QK_PAD]) * scale_a).astype(jnp.bfloat16)

    cn = _rms(c_kv, gkv_ref[...]).astype(jnp.bfloat16)
    kv = jnp.dot(cn, wkv_ref[...], preferred_element_type=jnp.float32)
    for hd in range(H_A):
        base = hd * QK_PAD
        ka_ref[:, base:base + QK_NOPE] = kv[:, hd * QK_NOPE:(hd + 1) * QK_NOPE].astype(jnp.bfloat16)
        ka_ref[:, base + QK_NOPE:base + QK_PAD] = k_rope
    vat_ref[...] = kv[:, H_A * QK_NOPE:].T.astype(jnp.bfloat16)

    qkv_b = jnp.dot(h, w_in_ref[:, 512:2048], preferred_element_type=jnp.float32)
    qb_ref[...] = (qkv_b[:, 0:1024] * scale_b).astype(jnp.bfloat16)
    lane = lax.broadcasted_iota(jnp.int32, (x.shape[0], LANES), 1)
    for g in range(KV_B):
        pair = qkv_b[:, 1024 + (g // 2) * LANES:1024 + (g // 2 + 1) * LANES]
        other = pltpu.roll(pair, HD_B, 1)
        first = (lane < HD_B) if g % 2 == 0 else (lane >= HD_B)
        rep = jnp.where(first, pair, other).astype(jnp.bfloat16)
        kb_ref[:, g * 2 * LANES:(g * 2 + 1) * LANES] = rep
        kb_ref[:, (g * 2 + 1) * LANES:(g * 2 + 2) * LANES] = rep
    vbt_ref[...] = qkv_b[:, 1280:1536].T.astype(jnp.bfloat16)

    gate_ref[...] = jnp.dot(h, w_in_ref[:, 2048:4096],
                            preferred_element_type=jnp.float32).astype(gate_ref.dtype)


def _input_projection(x2d, norm1_g, w_in_p, gq, wq_p, gkv, wkv_p, cos, sin, S):
    N = x2d.shape[0]
    tm = TM_PROJ
    s_tiles = S // tm
    row = lambda i: (i, 0)
    tab = lambda i: (i % s_tiles, 0)
    bf = jnp.bfloat16
    col = lambda i: (0, i)
    kvw = KV_B * HD_B
    outs = [((N, H_A * QK_PAD), (tm, H_A * QK_PAD), row, bf),
            ((N, H_A * QK_PAD), (tm, H_A * QK_PAD), row, bf),
            ((H_A * V_DIM, N), (H_A * V_DIM, tm), col, bf),
            ((N, H_B * HD_B), (tm, H_B * HD_B), row, bf),
            ((N, GROUP * kvw), (tm, GROUP * kvw), row, bf),
            ((kvw, N), (kvw, tm), col, bf),
            ((N, 2 * D_MODEL), (tm, 2 * D_MODEL), row, MIX_DTYPE)]
    return pl.pallas_call(
        functools.partial(_proj_kernel, scale_a=LOG2E / math.sqrt(QK_NOPE + QK_ROPE),
                          scale_b=LOG2E / math.sqrt(HD_B)),
        grid=(N // tm,),
        in_specs=[pl.BlockSpec((tm, D_MODEL), row),
                  _const_spec((1, D_MODEL)),
                  _const_spec(w_in_p.shape),
                  _const_spec((1, Q_LORA)),
                  _const_spec(wq_p.shape),
                  _const_spec((1, KV_LORA)),
                  _const_spec(wkv_p.shape),
                  pl.BlockSpec((tm, LANES), tab),
                  pl.BlockSpec((tm, LANES), tab)],
        out_specs=[pl.BlockSpec(blk, im) for _, blk, im, _ in outs],
        out_shape=[jax.ShapeDtypeStruct(shp, dt) for shp, _, _, dt in outs],
        compiler_params=_params(("parallel",)),
        name="input_projection",
    )(x2d, norm1_g, w_in_p, gq, wq_p, gkv, wkv_p, cos, sin)


def _mla_kernel(q_ref, k_ref, vt_ref, o_ref, bad_ref):
    f32 = jnp.float32
    n_chunks = k_ref.shape[1] // TK_MLA
    heads = range(MLA_HEADS)
    q = [q_ref[0, :, h * QK_PAD:(h + 1) * QK_PAD] for h in heads]


    def scores(h, c):
        kc = k_ref[0, c * TK_MLA:(c + 1) * TK_MLA, h * QK_PAD:(h + 1) * QK_PAD]
        return lax.dot_general(kc, q[h], (((1,), (1,)), ((), ())), preferred_element_type=f32)

    def pipeline(softmax, values):
        ahead = [[scores(h, c) for c in range(min(MLA_LOOKAHEAD, n_chunks))] for h in heads]
        pending = [None] * MLA_HEADS
        s = [None] * MLA_HEADS
        for c in range(n_chunks + 1):
            for h in heads:
                if c < n_chunks:
                    s[h] = ahead[h].pop(0)
                    if c + MLA_LOOKAHEAD < n_chunks:
                        ahead[h].append(scores(h, c + MLA_LOOKAHEAD))
            for h in heads:
                if pending[h] is not None:
                    values(h, c - 1, *pending[h])
                    pending[h] = None
            for h in heads:
                if c < n_chunks:
                    pending[h] = softmax(h, c, s[h])

    def first_pass():
        l = [None] * MLA_HEADS
        acc = [None] * MLA_HEADS

        def softmax(h, c, s):
            p = jnp.exp2(s)
            ps = jnp.sum(p, axis=0, keepdims=True)
            l[h] = ps if c == 0 else l[h] + ps
            return (p.astype(jnp.bfloat16),)

        def values(h, c, p):
            pv = jnp.dot(vt_ref[h * V_DIM:(h + 1) * V_DIM, c * TK_MLA:(c + 1) * TK_MLA], p,
                         preferred_element_type=f32)
            acc[h] = pv if acc[h] is None else acc[h] + pv

        pipeline(softmax, values)
        bad = 0.0
        for h in heads:
            o = acc[h] / l[h]
            bad = bad + _count_unsafe(o, l[h])
            o_ref[0, :, h * V_DIM:(h + 1) * V_DIM] = o.T.astype(o_ref.dtype)
        bad_ref[0] = bad

    first_pass()

    @pl.when(bad_ref[0] > 0.0)
    def _():
        m = [None] * MLA_HEADS
        l = [None] * MLA_HEADS
        acc = [None] * MLA_HEADS

        def softmax(h, c, s):
            mc = jnp.max(s, axis=0, keepdims=True)
            m_new = mc if c == 0 else jnp.maximum(m[h], mc)
            p = jnp.exp2(s - m_new)
            ps = jnp.sum(p, axis=0, keepdims=True)
            alpha = None if c == 0 else jnp.exp2(m[h] - m_new)
            l[h] = ps if c == 0 else alpha * l[h] + ps
            m[h] = m_new
            return p.astype(jnp.bfloat16), alpha

        def values(h, c, p, alpha):
            pv = jnp.dot(vt_ref[h * V_DIM:(h + 1) * V_DIM, c * TK_MLA:(c + 1) * TK_MLA], p,
                         preferred_element_type=f32)
            acc[h] = pv if alpha is None else alpha * acc[h] + pv

        pipeline(softmax, values)
        for h in heads:
            o_ref[0, :, h * V_DIM:(h + 1) * V_DIM] = (acc[h] / l[h]).T.astype(o_ref.dtype)


def _mla_attention(qa, ka, vat):
    B, S, _ = qa.shape
    tq = TQ_MLA
    nh = MLA_HEADS
    return pl.pallas_call(
        _mla_kernel,
        grid=(B, H_A // nh, S // tq),
        in_specs=[pl.BlockSpec((1, tq, nh * QK_PAD), lambda b, h, i: (b, i, h)),
                  pl.BlockSpec((1, S, nh * QK_PAD), lambda b, h, i: (b, 0, h)),
                  pl.BlockSpec((nh * V_DIM, S), lambda b, h, i: (h, b))],
        out_specs=pl.BlockSpec((1, tq, nh * V_DIM), lambda b, h, i: (b, i, h)),
        out_shape=jax.ShapeDtypeStruct((B, S, H_A * V_DIM), MIX_DTYPE),
        scratch_shapes=[pltpu.SMEM((1,), jnp.float32)],
        compiler_params=_params(("parallel", "parallel", "arbitrary")),
        name="mla_attention",
    )(qa, ka, vat)


def _window_kernel(q_ref, kp_ref, kc_ref, kn_ref, vp_ref, vc_ref, vn_ref, bias_ref, sink_ref, o_ref, ot_ref,
                   bad_ref, *, n_steps):
    step = pl.program_id(1)
    gw = GROUP * HD_B
    k_all = jnp.concatenate([kp_ref[0], kc_ref[0], kn_ref[0]], axis=0)
    vt_all = jnp.concatenate([vp_ref[...], vc_ref[...], vn_ref[...]], axis=1)
    lane = lax.broadcasted_iota(jnp.int32, (Q_BLOCK, gw), 1)
    work = [(i, g) for i in range(NQ_WIN) for g in range(KV_B)]

    def scores(t):
        i, g = work[t]
        qg = q_ref[0, i * Q_BLOCK:(i + 1) * Q_BLOCK, g * gw:(g + 1) * gw]
        wt = jnp.concatenate(
            [jnp.where((lane >= j * HD_B) & (lane < (j + 1) * HD_B), qg, jnp.zeros_like(qg))
             for j in range(GROUP)], axis=0)
        kg = k_all[i * Q_BLOCK:i * Q_BLOCK + SPAN, g * gw:(g + 1) * gw]
        return lax.dot_general(kg, wt, (((1,), (1,)), ((), ())), preferred_element_type=jnp.float32)

    def softmax(t, s, subtract_max):
        i, g = work[t]
        parts = []
        for part in range(3):
            sp = s[part * Q_BLOCK:(part + 1) * Q_BLOCK] + bias_ref[g, part]
            if part == 0 and i == 0:
                sp = jnp.where(step > 0, sp, NEG_INF)
            if part == 2 and i == NQ_WIN - 1:
                sp = jnp.where(step < n_steps - 1, sp, NEG_INF)
            parts.append(sp)
        s = jnp.concatenate(parts, axis=0)
        sink = sink_ref[g]
        if subtract_max:
            m = jnp.maximum(jnp.max(s, axis=0, keepdims=True), sink)
            s, sink = s - m, sink - m
        p = jnp.exp2(s)
        l = jnp.sum(p, axis=0, keepdims=True) + jnp.exp2(sink)
        return p.astype(jnp.bfloat16), l

    def values(t, p, l):
        i, g = work[t]
        vt = vt_all[g * HD_B:(g + 1) * HD_B, i * Q_BLOCK:i * Q_BLOCK + SPAN]
        r = jnp.dot(vt, p, preferred_element_type=jnp.float32) / l
        for j in range(GROUP):
            hd = g * GROUP + j
            ot_ref[hd * HD_B:(hd + 1) * HD_B, i * Q_BLOCK:(i + 1) * Q_BLOCK] = r[:, j * Q_BLOCK:(j + 1) * Q_BLOCK]
        return _count_unsafe(r, l)

    def pipeline(subtract_max):
        s_next = scores(0)
        pending = None
        bad = 0.0
        for t in range(len(work) + 1):
            if t < len(work):
                s = s_next
                if t + 1 < len(work):
                    s_next = scores(t + 1)
            if pending is not None:
                bad = bad + values(*pending)
            if t < len(work):
                pending = (t,) + softmax(t, s, subtract_max)
        return bad

    bad_ref[0] = pipeline(subtract_max=False)

    @pl.when(bad_ref[0] > 0.0)
    def _():
        pipeline(subtract_max=True)

    o_ref[0] = ot_ref[...].T.astype(o_ref.dtype)


def _window_attention(qb, kb_rep, vbt, bias, sink_rows):
    B, S, _ = qb.shape
    nblk = S // Q_BLOCK
    rows = NQ_WIN * Q_BLOCK
    n_steps = S // rows
    w = H_B * HD_B
    kprev = lambda b, s: (b, jnp.maximum(s * NQ_WIN - 1, 0), 0)
    cur = lambda b, s: (b, s, 0)
    knext = lambda b, s: (b, jnp.minimum((s + 1) * NQ_WIN, nblk - 1), 0)
    vprev = lambda b, s: (0, b * nblk + jnp.maximum(s * NQ_WIN - 1, 0))
    vcur = lambda b, s: (0, b * n_steps + s)
    vnext = lambda b, s: (0, b * nblk + jnp.minimum((s + 1) * NQ_WIN, nblk - 1))
    kvw = KV_B * HD_B
    return pl.pallas_call(
        functools.partial(_window_kernel, n_steps=n_steps),
        grid=(B, n_steps),
        in_specs=[pl.BlockSpec((1, rows, w), cur),
                  pl.BlockSpec((1, Q_BLOCK, w), kprev),
                  pl.BlockSpec((1, rows, w), cur),
                  pl.BlockSpec((1, Q_BLOCK, w), knext),
                  pl.BlockSpec((kvw, Q_BLOCK), vprev),
                  pl.BlockSpec((kvw, rows), vcur),
                  pl.BlockSpec((kvw, Q_BLOCK), vnext),
                  _const_spec(bias.shape),
                  _const_spec(sink_rows.shape)],
        out_specs=pl.BlockSpec((1, rows, w), cur),
        out_shape=jax.ShapeDtypeStruct((B, S, w), MIX_DTYPE),
        scratch_shapes=[pltpu.VMEM((w, rows), jnp.float32), pltpu.SMEM((1,), jnp.float32)],
        compiler_params=_params(("parallel", "arbitrary")),
        name="window_attention",
    )(qb, kb_rep, kb_rep, kb_rep, vbt, vbt, vbt, bias, sink_rows)


def _mix_ffn_kernel(xp_ref, xc_ref, xn_ref, ap_ref, ac_ref, an_ref, bp_ref, bc_ref, bn_ref,
                    gp_ref, gc_ref, gn_ref, wout_ref, g2_ref, wup_ref, cw_ref, cb_ref, wdn_ref, gf_ref,
                    o_ref, act_ref, *, n_tiles, final_norm):
    i = pl.program_id(1)
    tm = xc_ref.shape[1]
    f32 = jnp.float32

    def ext(p_ref, c_ref, n_ref):
        return jnp.concatenate([p_ref[0], c_ref[0], n_ref[0]], axis=0)

    gates = ext(gp_ref, gc_ref, gn_ref)
    mixed = (jax.nn.sigmoid(gates[:, 0:D_MODEL].astype(f32)) * ext(ap_ref, ac_ref, an_ref).astype(f32)
             + jax.nn.sigmoid(gates[:, D_MODEL:2 * D_MODEL].astype(f32)) * ext(bp_ref, bc_ref, bn_ref).astype(f32))
    x1e = ext(xp_ref, xc_ref, xn_ref) + jnp.dot(mixed.astype(jnp.bfloat16), wout_ref[...],
                                                preferred_element_type=f32)
    x1 = x1e[HALO:HALO + tm]
    h2e = _rms(x1e, g2_ref[...]).astype(jnp.bfloat16)
    hp = jnp.where(i > 0, h2e[0:HALO], jnp.zeros((HALO, D_MODEL), jnp.bfloat16))
    hn = jnp.where(i < n_tiles - 1, h2e[HALO + tm:], jnp.zeros((HALO, D_MODEL), jnp.bfloat16))
    he = jnp.concatenate([hp, h2e[HALO:HALO + tm], hn], axis=0)
    n_chunks = D_FF // FF_CHUNK

    def cols(ref, j):
        g0 = j * FF_CHUNK
        return jnp.concatenate([ref[:, g0:g0 + FF_CHUNK], ref[:, D_FF + g0:D_FF + g0 + FF_CHUNK]], axis=1)

    def up(j):
        return jnp.dot(he, cols(wup_ref, j), preferred_element_type=jnp.float32)

    u_next = up(0)
    for j in range(n_chunks):
        u = u_next
        if j + 1 < n_chunks:
            u_next = up(j + 1)
        cw = cols(cw_ref, j)
        cb = cols(cb_ref, j)
        y = (u[HALO - 1:HALO - 1 + tm] * cw[0:1] + u[HALO:HALO + tm] * cw[1:2]
             + u[HALO + 1:HALO + 1 + tm] * cw[2:3] + cb)
        gate = y[:, 0:FF_CHUNK]
        val = y[:, FF_CHUNK:2 * FF_CHUNK]
        act_ref[:, j * FF_CHUNK:(j + 1) * FF_CHUNK] = (jax.nn.silu(gate) * val).astype(jnp.bfloat16)
    x2 = x1 + jnp.dot(act_ref[...], wdn_ref[...], preferred_element_type=jnp.float32)
    o_ref[0] = _rms(x2, gf_ref[...]) if final_norm else x2


def _mix_ffn(x, oa, ob, gates, w_out, norm2_g, wup, cw, cb, wdn, final_g, final_norm):
    B, S, _ = x.shape
    tm = TM_FFN
    n_tiles = S // tm
    hb = tm // HALO
    cur = lambda b, i: (b, i, 0)
    prev = lambda b, i: (b, jnp.maximum(i * hb - 1, 0), 0)
    nxt = lambda b, i: (b, jnp.minimum((i + 1) * hb, S // HALO - 1), 0)

    def halo_specs(width):
        return [pl.BlockSpec((1, HALO, width), prev), pl.BlockSpec((1, tm, width), cur),
                pl.BlockSpec((1, HALO, width), nxt)]

    return pl.pallas_call(
        functools.partial(_mix_ffn_kernel, n_tiles=n_tiles, final_norm=final_norm),
        grid=(B, n_tiles),
        in_specs=(halo_specs(D_MODEL) + halo_specs(D_MODEL) + halo_specs(D_MODEL) + halo_specs(2 * D_MODEL)
                  + [_const_spec(w_out.shape), _const_spec((1, D_MODEL)), _const_spec(wup.shape),
                     _const_spec(cw.shape), _const_spec(cb.shape), _const_spec(wdn.shape),
                     _const_spec((1, D_MODEL))]),
        out_specs=pl.BlockSpec((1, tm, D_MODEL), cur),
        out_shape=jax.ShapeDtypeStruct((B, S, D_MODEL), jnp.float32),
        scratch_shapes=[pltpu.VMEM((tm, D_FF), jnp.bfloat16)],
        compiler_params=_params(("parallel", "arbitrary")),
        name="mix_ffn",
    )(x, x, x, oa, oa, oa, ob, ob, ob, gates, gates, gates, w_out, norm2_g, wup, cw, cb, wdn, final_g)


def _prep_w_in(w):
    pad = jnp.zeros((D_MODEL, LANES - QK_ROPE), w.dtype)
    split = Q_LORA + KV_LORA + QK_ROPE
    return jnp.concatenate([w[:, :split], pad, w[:, split:]], axis=1).astype(jnp.bfloat16)


def _prep_w_q_b(w):
    w3 = w.reshape(Q_LORA, H_A, QK_NOPE + QK_ROPE)
    pad = jnp.zeros((Q_LORA, H_A, QK_PAD - QK_NOPE - QK_ROPE), w.dtype)
    return jnp.concatenate([w3, pad], axis=-1).reshape(Q_LORA, H_A * QK_PAD).astype(jnp.bfloat16)


def _prep_w_kv_b(w):
    w3 = w.reshape(KV_LORA, H_A, QK_NOPE + V_DIM)
    k = w3[:, :, :QK_NOPE].reshape(KV_LORA, H_A * QK_NOPE)
    v = w3[:, :, QK_NOPE:].reshape(KV_LORA, H_A * V_DIM)
    return jnp.concatenate([k, v], axis=1).astype(jnp.bfloat16)


def kernel(x, positions, norm1_g, w_in, q_a_norm_g, w_q_b, kv_a_norm_g, w_kv_b, rel_bias, sinks,
           w_out, norm2_g, w_up, conv_w, conv_b, w_down, final_norm_g):
    B, S, D = x.shape
    depth = norm1_g.shape[0]
    cos, sin = _rope_tables(positions)
    bias = _window_bias(rel_bias)
    f32 = jnp.float32
    for l in range(depth):
        x2d = x.reshape(B * S, D)
        qa, ka, vat, qb, kb_rep, vbt, gates = _input_projection(
            x2d, norm1_g[l][None].astype(f32), _prep_w_in(w_in[l]),
            q_a_norm_g[l][None].astype(f32), _prep_w_q_b(w_q_b[l]),
            kv_a_norm_g[l][None].astype(f32), _prep_w_kv_b(w_kv_b[l]), cos, sin, S)
        r3 = lambda a: a.reshape(B, S, a.shape[-1])
        oa = _mla_attention(r3(qa), r3(ka), vat)
        sink_rows = jnp.repeat(sinks[l].astype(f32) * LOG2E, Q_BLOCK).reshape(KV_B, 1, GROUP * Q_BLOCK)
        ob = _window_attention(r3(qb), r3(kb_rep), vbt, bias, sink_rows)
        x = _mix_ffn(x, oa, ob, r3(gates), w_out[l].astype(jnp.bfloat16), norm2_g[l][None].astype(f32),
                     w_up[l].astype(jnp.bfloat16), conv_w[l].astype(f32), conv_b[l][None].astype(f32),
                     w_down[l].astype(jnp.bfloat16), final_norm_g[None].astype(f32),
                     final_norm=(l == depth - 1))
    return x
```

```python
import functools
import math

import jax
import jax.numpy as jnp
import numpy as np
from jax import lax
from jax.experimental import pallas as pl
from jax.experimental.pallas import tpu as pltpu

D_MODEL = 1024
EPS = 1e-6
Q_BLOCK = 128
H_A = 8
QK_NOPE = 128
QK_ROPE = 64
V_DIM = 128
Q_LORA = 256
KV_LORA = 128
ROPE_THETA = 10000.0
H_B = 16
KV_B = 4
GROUP = H_B // KV_B
HD_B = 64
WINDOW = 128
NUM_BUCKETS = 32
MAX_DISTANCE = 128
D_FF = 2816
SPAN = Q_BLOCK + 2 * WINDOW

LANES = 128
QK_PAD = 256
NEG_INF = -1e30
LOG2E = math.log2(math.e)
MIX_DTYPE = jnp.bfloat16
MIN_DENOM = 2.0 ** -60
VMEM_LIMIT = 56 * 1024 * 1024

TM_PROJ = 512
TQ_MLA = 512
TK_MLA = 1024
MLA_LOOKAHEAD = 1
MLA_HEADS = 2
NQ_WIN = 4
WIN_LOOKAHEAD = 1
TM_FFN = 512
FF_CHUNK = 256
HALO = 16


def _const_spec(shape):
    nd = len(shape)
    return pl.BlockSpec(shape, lambda *_: (0,) * nd, pipeline_mode=pl.Buffered(1))


def _params(sem):
    return pltpu.CompilerParams(dimension_semantics=sem, vmem_limit_bytes=VMEM_LIMIT)


def _rms(x, g):
    return x * lax.rsqrt(jnp.mean(x * x, axis=-1, keepdims=True) + EPS) * g


def _count_unsafe(o, l):
    col = jnp.sum(o, axis=0, keepdims=True)
    bad_o = jnp.where(jnp.isfinite(col), 0.0, 1.0)
    bad_l = jnp.where(jnp.isfinite(l) & (l >= MIN_DENOM), 0.0, 1.0)
    return jnp.sum(bad_o) + jnp.sum(bad_l)


def _swap_halves_32(x):
    lane = lax.broadcasted_iota(jnp.int32, x.shape, 1)
    up = pltpu.roll(x, LANES - 32, 1)
    down = pltpu.roll(x, 32, 1)
    return jnp.where((lane & 32) == 0, up, down)


def _rope_table_kernel(pos_ref, inv_ref, sign_ref, cos_ref, sin_ref):
    ang = pos_ref[...] * inv_ref[...]
    cos_ref[...] = jnp.cos(ang)
    sin_ref[...] = jnp.sin(ang) * sign_ref[...]


def _rope_tables(positions):
    S = positions.shape[0]
    half = QK_ROPE // 2
    inv_freq = ROPE_THETA ** (-jnp.arange(half, dtype=jnp.float32) / half)
    inv = jnp.tile(inv_freq, LANES // half)[None, :]
    sign = jnp.tile(jnp.concatenate([-jnp.ones(half, jnp.float32), jnp.ones(half, jnp.float32)]),
                    LANES // QK_ROPE)[None, :]
    pos = positions.astype(jnp.float32)[:, None]
    ts = 512
    return pl.pallas_call(
        _rope_table_kernel,
        grid=(S // ts,),
        in_specs=[pl.BlockSpec((ts, 1), lambda i: (i, 0)),
                  pl.BlockSpec((1, LANES), lambda i: (0, 0)),
                  pl.BlockSpec((1, LANES), lambda i: (0, 0))],
        out_specs=[pl.BlockSpec((ts, LANES), lambda i: (i, 0)),
                   pl.BlockSpec((ts, LANES), lambda i: (i, 0))],
        out_shape=[jax.ShapeDtypeStruct((S, LANES), jnp.float32)] * 2,
        compiler_params=_params(("parallel",)),
        name="rope_tables",
    )(pos, inv, sign)


def _t5_bucket(rel):
    nb = NUM_BUCKETS // 2
    max_exact = nb // 2
    base = (rel > 0).astype(jnp.int32) * nb
    n = jnp.abs(rel)
    nf = jnp.maximum(n, 1).astype(jnp.float32)
    large = max_exact + (jnp.log(nf / max_exact) / math.log(MAX_DISTANCE / max_exact)
                         * (nb - max_exact)).astype(jnp.int32)
    large = jnp.minimum(large, nb - 1)
    return base + jnp.where(n < max_exact, n, large)


def _bias_kernel(tab_ref, band_ref, out_ref):
    band = band_ref[0]
    for j in range(GROUP):
        out_ref[0, 0, :, j * Q_BLOCK:(j + 1) * Q_BLOCK] = jnp.where(band != 0, tab_ref[j, 0] * LOG2E, NEG_INF)


def _window_bias(rel_bias):
    a = jnp.arange(Q_BLOCK, dtype=jnp.int32)[None, :]
    c = jnp.arange(SPAN, dtype=jnp.int32)[:, None]
    rel = c - WINDOW - a
    tab = rel_bias[_t5_bucket(rel)].astype(jnp.float32)
    tab = tab.transpose(2, 0, 1).reshape(H_B, 3, Q_BLOCK, Q_BLOCK)
    band = (jnp.abs(rel) <= WINDOW).astype(jnp.int32).reshape(3, Q_BLOCK, Q_BLOCK)
    return pl.pallas_call(
        _bias_kernel,
        grid=(KV_B, 3),
        in_specs=[pl.BlockSpec((GROUP, 1, Q_BLOCK, Q_BLOCK), lambda g, p: (g, p, 0, 0)),
                  pl.BlockSpec((1, Q_BLOCK, Q_BLOCK), lambda g, p: (p, 0, 0))],
        out_specs=pl.BlockSpec((1, 1, Q_BLOCK, GROUP * Q_BLOCK), lambda g, p: (g, p, 0, 0)),
        out_shape=jax.ShapeDtypeStruct((KV_B, 3, Q_BLOCK, GROUP * Q_BLOCK), jnp.float32),
        compiler_params=_params(("arbitrary", "arbitrary")),
        name="window_bias",
    )(tab, band)


def _proj_kernel(x_ref, g1_ref, w_in_ref, gq_ref, wq_ref, gkv_ref, wkv_ref, cos_ref, sin_ref,
                 qa_ref, ka_ref, vat_ref, qb_ref, kb_ref, vbt_ref, gate_ref, *, scale_a, scale_b):
    x = x_ref[...]
    h = _rms(x, g1_ref[...]).astype(jnp.bfloat16)
    cos = cos_ref[...]
    sin = sin_ref[...]

    def rope(blk):
        return blk * cos + _swap_halves_32(blk) * sin

    lat = jnp.dot(h, w_in_ref[:, 0:512], preferred_element_type=jnp.float32)
    q_lat = lat[:, 0:Q_LORA]
    c_kv = lat[:, Q_LORA:Q_LORA + KV_LORA]
    k_rope = rope(lat[:, 384:512]).astype(jnp.bfloat16)

    qn = _rms(q_lat, gq_ref[...]).astype(jnp.bfloat16)
    q = jnp.dot(qn, wq_ref[...], preferred_element_type=jnp.float32)
    for hd in range(H_A):
        base = hd * QK_PAD
        qa_ref[:, base:base + QK_NOPE] = (q[:, base:base + QK_NOPE] * scale_a).astype(jnp.bfloat16)
        qa_ref[:, base + QK_NOPE:base + QK_PAD] = (
            rope(q[:, base + QK_NOPE:base + QK_PAD]) * scale_a).astype(jnp.bfloat16)

    cn = _rms(c_kv, gkv_ref[...]).astype(jnp.bfloat16)
    kv = jnp.dot(cn, wkv_ref[...], preferred_element_type=jnp.float32)
    for hd in range(H_A):
        base = hd * QK_PAD
        ka_ref[:, base:base + QK_NOPE] = kv[:, hd * QK_NOPE:(hd + 1) * QK_NOPE].astype(jnp.bfloat16)
        ka_ref[:, base + QK_NOPE:base + QK_PAD] = k_rope
    vat_ref[...] = kv[:, H_A * QK_NOPE:].T.astype(jnp.bfloat16)

    qkv_b = jnp.dot(h, w_in_ref[:, 512:2048], preferred_element_type=jnp.float32)
    qb_ref[...] = (qkv_b[:, 0:1024] * scale_b).astype(jnp.bfloat16)
    lane = lax.broadcasted_iota(jnp.int32, (x.shape[0], LANES), 1)
    for g in range(KV_B):
        pair = qkv_b[:, 1024 + (g // 2) * LANES:1024 + (g // 2 + 1) * LANES]
        other = pltpu.roll(pair, HD_B, 1)
        first = (lane < HD_B) if g % 2 == 0 else (lane >= HD_B)
        rep = jnp.where(first, pair, other).astype(jnp.bfloat16)
        kb_ref[:, g * 2 * LANES:(g * 2 + 1) * LANES] = rep
        kb_ref[:, (g * 2 + 1) * LANES:(g * 2 + 2) * LANES] = rep
    vbt_ref[...] = qkv_b[:, 1280:1536].T.astype(jnp.bfloat16)

    gate_ref[...] = jnp.dot(h, w_in_ref[:, 2048:4096],
                            preferred_element_type=jnp.float32).astype(gate_ref.dtype)


def _input_projection(x2d, norm1_g, w_in_p, gq, wq_p, gkv, wkv_p, cos, sin, S):
    N = x2d.shape[0]
    tm = TM_PROJ
    s_tiles = S // tm
    row = lambda i: (i, 0)
    tab = lambda i: (i % s_tiles, 0)
    bf = jnp.bfloat16
    col = lambda i: (0, i)
    kvw = KV_B * HD_B
    outs = [((N, H_A * QK_PAD), (tm, H_A * QK_PAD), row, bf),
            ((N, H_A * QK_PAD), (tm, H_A * QK_PAD), row, bf),
            ((H_A * V_DIM, N), (H_A * V_DIM, tm), col, bf),
            ((N, H_B * HD_B), (tm, H_B * HD_B), row, bf),
            ((N, GROUP * kvw), (tm, GROUP * kvw), row, bf),
            ((kvw, N), (kvw, tm), col, bf),
            ((N, 2 * D_MODEL), (tm, 2 * D_MODEL), row, MIX_DTYPE)]
    return pl.pallas_call(
        functools.partial(_proj_kernel, scale_a=LOG2E / math.sqrt(QK_NOPE + QK_ROPE),
                          scale_b=LOG2E / math.sqrt(HD_B)),
        grid=(N // tm,),
        in_specs=[pl.BlockSpec((tm, D_MODEL), row),
                  _const_spec((1, D_MODEL)),
                  _const_spec(w_in_p.shape),
                  _const_spec((1, Q_LORA)),
                  _const_spec(wq_p.shape),
                  _const_spec((1, KV_LORA)),
                  _const_spec(wkv_p.shape),
                  pl.BlockSpec((tm, LANES), tab),
                  pl.BlockSpec((tm, LANES), tab)],
        out_specs=[pl.BlockSpec(blk, im) for _, blk, im, _ in outs],
        out_shape=[jax.ShapeDtypeStruct(shp, dt) for shp, _, _, dt in outs],
        compiler_params=_params(("parallel",)),
        name="input_projection",
    )(x2d, norm1_g, w_in_p, gq, wq_p, gkv, wkv_p, cos, sin)


def _mla_kernel(q_ref, k_ref, vt_ref, o_ref, bad_ref):
    f32 = jnp.float32
    n_chunks = k_ref.shape[1] // TK_MLA
    heads = range(MLA_HEADS)
    q = [q_ref[0, :, h * QK_PAD:(h + 1) * QK_PAD] for h in heads]


    def scores(h, c):
        kc = k_ref[0, c * TK_MLA:(c + 1) * TK_MLA, h * QK_PAD:(h + 1) * QK_PAD]
        return lax.dot_general(kc, q[h], (((1,), (1,)), ((), ())), preferred_element_type=f32)

    def pipeline(softmax, values):
        ahead = [[scores(h, c) for c in range(min(MLA_LOOKAHEAD, n_chunks))] for h in heads]
        pending = [None] * MLA_HEADS
        s = [None] * MLA_HEADS
        for c in range(n_chunks + 1):
            for h in heads:
                if c < n_chunks:
                    s[h] = ahead[h].pop(0)
                    if c + MLA_LOOKAHEAD < n_chunks:
                        ahead[h].append(scores(h, c + MLA_LOOKAHEAD))
            for h in heads:
                if pending[h] is not None:
                    values(h, c - 1, *pending[h])
                    pending[h] = None
            for h in heads:
                if c < n_chunks:
                    pending[h] = softmax(h, c, s[h])

    def first_pass():
        l = [None] * MLA_HEADS
        acc = [None] * MLA_HEADS

        def softmax(h, c, s):
            p = jnp.exp2(s)
            ps = jnp.sum(p, axis=0, keepdims=True)
            l[h] = ps if c == 0 else l[h] + ps
            return (p.astype(jnp.bfloat16),)

        def values(h, c, p):
            pv = jnp.dot(vt_ref[h * V_DIM:(h + 1) * V_DIM, c * TK_MLA:(c + 1) * TK_MLA], p,
                         preferred_element_type=f32)
            acc[h] = pv if acc[h] is None else acc[h] + pv

        pipeline(softmax, values)
        bad = 0.0
        for h in heads:
            o = acc[h] / l[h]
            bad = bad + _count_unsafe(o, l[h])
            o_ref[0, :, h * V_DIM:(h + 1) * V_DIM] = o.T.astype(o_ref.dtype)
        bad_ref[0] = bad

    first_pass()

    @pl.when(bad_ref[0] > 0.0)
    def _():
        m = [None] * MLA_HEADS
        l = [None] * MLA_HEADS
        acc = [None] * MLA_HEADS

        def softmax(h, c, s):
            mc = jnp.max(s, axis=0, keepdims=True)
            m_new = mc if c == 0 else jnp.maximum(m[h], mc)
            p = jnp.exp2(s - m_new)
            ps = jnp.sum(p, axis=0, keepdims=True)
            alpha = None if c == 0 else jnp.exp2(m[h] - m_new)
            l[h] = ps if c == 0 else alpha * l[h] + ps
            m[h] = m_new
            return p.astype(jnp.bfloat16), alpha

        def values(h, c, p, alpha):
            pv = jnp.dot(vt_ref[h * V_DIM:(h + 1) * V_DIM, c * TK_MLA:(c + 1) * TK_MLA], p,
                         preferred_element_type=f32)
            acc[h] = pv if alpha is None else alpha * acc[h] + pv

        pipeline(softmax, values)
        for h in heads:
            o_ref[0, :, h * V_DIM:(h + 1) * V_DIM] = (acc[h] / l[h]).T.astype(o_ref.dtype)


def _mla_attention(qa, ka, vat):
    B, S, _ = qa.shape
    tq = TQ_MLA
    nh = MLA_HEADS
    return pl.pallas_call(
        _mla_kernel,
        grid=(B, H_A // nh, S // tq),
        in_specs=[pl.BlockSpec((1, tq, nh * QK_PAD), lambda b, h, i: (b, i, h)),
                  pl.BlockSpec((1, S, nh * QK_PAD), lambda b, h, i: (b, 0, h)),
                  pl.BlockSpec((nh * V_DIM, S), lambda b, h, i: (h, b))],
        out_specs=pl.BlockSpec((1, tq, nh * V_DIM), lambda b, h, i: (b, i, h)),
        out_shape=jax.ShapeDtypeStruct((B, S, H_A * V_DIM), MIX_DTYPE),
        scratch_shapes=[pltpu.SMEM((1,), jnp.float32)],
        compiler_params=_params(("parallel", "parallel", "arbitrary")),
        name="mla_attention",
    )(qa, ka, vat)


def _window_kernel(q_ref, kp_ref, kc_ref, kn_ref, vp_ref, vc_ref, vn_ref, bias_ref, sink_ref, o_ref, ot_ref,
                   bad_ref, *, n_steps):
    step = pl.program_id(1)
    gw = GROUP * HD_B
    lane = lax.broadcasted_iota(jnp.int32, (Q_BLOCK, gw), 1)
    work = [(i, g) for i in range(NQ_WIN) for g in range(KV_B)]

    def key_blocks(i):
        blocks = []
        for blk in (i - 1, i, i + 1):
            if blk < 0:
                blocks.append(("prev", 0))
            elif blk >= NQ_WIN:
                blocks.append(("next", 0))
            else:
                blocks.append(("cur", blk))
        return blocks

    def k_slab(i, g):
        refs = {"prev": kp_ref, "cur": kc_ref, "next": kn_ref}
        return jnp.concatenate([refs[kind][0, o * Q_BLOCK:(o + 1) * Q_BLOCK, g * gw:(g + 1) * gw]
                                for kind, o in key_blocks(i)], axis=0)

    def vt_slab(i, g):
        refs = {"prev": vp_ref, "cur": vc_ref, "next": vn_ref}
        return jnp.concatenate([refs[kind][g * HD_B:(g + 1) * HD_B, o * Q_BLOCK:(o + 1) * Q_BLOCK]
                                for kind, o in key_blocks(i)], axis=1)

    def scores(t):
        i, g = work[t]
        qg = q_ref[0, i * Q_BLOCK:(i + 1) * Q_BLOCK, g * gw:(g + 1) * gw]
        wt = jnp.concatenate(
            [jnp.where((lane >= j * HD_B) & (lane < (j + 1) * HD_B), qg, jnp.zeros_like(qg))
             for j in range(GROUP)], axis=0)
        return lax.dot_general(k_slab(i, g), wt, (((1,), (1,)), ((), ())), preferred_element_type=jnp.float32)

    def softmax(t, s, subtract_max):
        i, g = work[t]
        parts = []
        for part in range(3):
            sp = s[part * Q_BLOCK:(part + 1) * Q_BLOCK] + bias_ref[g, part]
            if part == 0 and i == 0:
                sp = jnp.where(step > 0, sp, NEG_INF)
            if part == 2 and i == NQ_WIN - 1:
                sp = jnp.where(step < n_steps - 1, sp, NEG_INF)
            parts.append(sp)
        s = jnp.concatenate(parts, axis=0)
        sink = sink_ref[g]
        if subtract_max:
            m = jnp.maximum(jnp.max(s, axis=0, keepdims=True), sink)
            s, sink = s - m, sink - m
        p = jnp.exp2(s)
        l = jnp.sum(p, axis=0, keepdims=True) + jnp.exp2(sink)
        return p.astype(jnp.bfloat16), l

    def values(t, p, l):
        i, g = work[t]
        r = jnp.dot(vt_slab(i, g), p, preferred_element_type=jnp.float32) / l
        for j in range(GROUP):
            hd = g * GROUP + j
            ot_ref[hd * HD_B:(hd + 1) * HD_B, i * Q_BLOCK:(i + 1) * Q_BLOCK] = r[:, j * Q_BLOCK:(j + 1) * Q_BLOCK]
        if g == KV_B - 1:
            o_ref[0, i * Q_BLOCK:(i + 1) * Q_BLOCK, :] = (
                ot_ref[:, i * Q_BLOCK:(i + 1) * Q_BLOCK].T.astype(o_ref.dtype))
        return _count_unsafe(r, l)

    def pipeline(subtract_max):
        ahead = [scores(t) for t in range(WIN_LOOKAHEAD)]
        pending = None
        bad = 0.0
        for t in range(len(work) + 1):
            if t < len(work):
                s = ahead.pop(0)
                if t + WIN_LOOKAHEAD < len(work):
                    ahead.append(scores(t + WIN_LOOKAHEAD))
            if pending is not None:
                bad = bad + values(*pending)
            if t < len(work):
                pending = (t,) + softmax(t, s, subtract_max)
        return bad

    bad_ref[0] = pipeline(subtract_max=False)

    @pl.when(bad_ref[0] > 0.0)
    def _():
        pipeline(subtract_max=True)


def _window_attention(qb, kb_rep, vbt, bias, sink_rows):
    B, S, _ = qb.shape
    nblk = S // Q_BLOCK
    rows = NQ_WIN * Q_BLOCK
    n_steps = S // rows
    w = H_B * HD_B
    kprev = lambda b, s: (b, jnp.maximum(s * NQ_WIN - 1, 0), 0)
    cur = lambda b, s: (b, s, 0)
    knext = lambda b, s: (b, jnp.minimum((s + 1) * NQ_WIN, nblk - 1), 0)
    vprev = lambda b, s: (0, b * nblk + jnp.maximum(s * NQ_WIN - 1, 0))
    vcur = lambda b, s: (0, b * n_steps + s)
    vnext = lambda b, s: (0, b * nblk + jnp.minimum((s + 1) * NQ_WIN, nblk - 1))
    kvw = KV_B * HD_B
    return pl.pallas_call(
        functools.partial(_window_kernel, n_steps=n_steps),
        grid=(B, n_steps),
        in_specs=[pl.BlockSpec((1, rows, w), cur),
                  pl.BlockSpec((1, Q_BLOCK, w), kprev),
                  pl.BlockSpec((1, rows, w), cur),
                  pl.BlockSpec((1, Q_BLOCK, w), knext),
                  pl.BlockSpec((kvw, Q_BLOCK), vprev),
                  pl.BlockSpec((kvw, rows), vcur),
                  pl.BlockSpec((kvw, Q_BLOCK), vnext),
                  _const_spec(bias.shape),
                  _const_spec(sink_rows.shape)],
        out_specs=pl.BlockSpec((1, rows, w), cur),
        out_shape=jax.ShapeDtypeStruct((B, S, w), MIX_DTYPE),
        scratch_shapes=[pltpu.VMEM((w, rows), jnp.float32), pltpu.SMEM((1,), jnp.float32)],
        compiler_params=_params(("parallel", "arbitrary")),
        name="window_attention",
    )(qb, kb_rep, kb_rep, kb_rep, vbt, vbt, vbt, bias, sink_rows)


def _mix_ffn_kernel(xp_ref, xc_ref, xn_ref, ap_ref, ac_ref, an_ref, bp_ref, bc_ref, bn_ref,
                    gp_ref, gc_ref, gn_ref, wout_ref, g2_ref, wup_ref, cw_ref, cb_ref, wdn_ref, gf_ref,
                    o_ref, act_ref, *, n_tiles, final_norm):
    i = pl.program_id(1)
    tm = xc_ref.shape[1]
    f32 = jnp.float32

    def ext(p_ref, c_ref, n_ref):
        return jnp.concatenate([p_ref[0], c_ref[0], n_ref[0]], axis=0)

    gates = ext(gp_ref, gc_ref, gn_ref)
    mixed = (jax.nn.sigmoid(gates[:, 0:D_MODEL].astype(f32)) * ext(ap_ref, ac_ref, an_ref).astype(f32)
             + jax.nn.sigmoid(gates[:, D_MODEL:2 * D_MODEL].astype(f32)) * ext(bp_ref, bc_ref, bn_ref).astype(f32))
    x1e = ext(xp_ref, xc_ref, xn_ref) + jnp.dot(mixed.astype(jnp.bfloat16), wout_ref[...],
                                                preferred_element_type=f32)
    x1 = x1e[HALO:HALO + tm]
    h2e = _rms(x1e, g2_ref[...]).astype(jnp.bfloat16)
    hp = jnp.where(i > 0, h2e[0:HALO], jnp.zeros((HALO, D_MODEL), jnp.bfloat16))
    hn = jnp.where(i < n_tiles - 1, h2e[HALO + tm:], jnp.zeros((HALO, D_MODEL), jnp.bfloat16))
    he = jnp.concatenate([hp, h2e[HALO:HALO + tm], hn], axis=0)
    n_chunks = D_FF // FF_CHUNK

    def cols(ref, j):
        g0 = j * FF_CHUNK
        return jnp.concatenate([ref[:, g0:g0 + FF_CHUNK], ref[:, D_FF + g0:D_FF + g0 + FF_CHUNK]], axis=1)

    def up(j):
        return jnp.dot(he, cols(wup_ref, j), preferred_element_type=jnp.float32)

    u_next = up(0)
    for j in range(n_chunks):
        u = u_next
        if j + 1 < n_chunks:
            u_next = up(j + 1)
        cw = cols(cw_ref, j)
        cb = cols(cb_ref, j)
        y = (u[HALO - 1:HALO - 1 + tm] * cw[0:1] + u[HALO:HALO + tm] * cw[1:2]
             + u[HALO + 1:HALO + 1 + tm] * cw[2:3] + cb)
        gate = y[:, 0:FF_CHUNK]
        val = y[:, FF_CHUNK:2 * FF_CHUNK]
        act_ref[:, j * FF_CHUNK:(j + 1) * FF_CHUNK] = (jax.nn.silu(gate) * val).astype(jnp.bfloat16)
    x2 = x1 + jnp.dot(act_ref[...], wdn_ref[...], preferred_element_type=jnp.float32)
    o_ref[0] = _rms(x2, gf_ref[...]) if final_norm else x2


def _mix_ffn(x, oa, ob, gates, w_out, norm2_g, wup, cw, cb, wdn, final_g, final_norm):
    B, S, _ = x.shape
    tm = TM_FFN
    n_tiles = S // tm
    hb = tm // HALO
    cur = lambda b, i: (b, i, 0)
    prev = lambda b, i: (b, jnp.maximum(i * hb - 1, 0), 0)
    nxt = lambda b, i: (b, jnp.minimum((i + 1) * hb, S // HALO - 1), 0)

    def halo_specs(width):
        return [pl.BlockSpec((1, HALO, width), prev), pl.BlockSpec((1, tm, width), cur),
                pl.BlockSpec((1, HALO, width), nxt)]

    return pl.pallas_call(
        functools.partial(_mix_ffn_kernel, n_tiles=n_tiles, final_norm=final_norm),
        grid=(B, n_tiles),
        in_specs=(halo_specs(D_MODEL) + halo_specs(D_MODEL) + halo_specs(D_MODEL) + halo_specs(2 * D_MODEL)
                  + [_const_spec(w_out.shape), _const_spec((1, D_MODEL)), _const_spec(wup.shape),
                     _const_spec(cw.shape), _const_spec(cb.shape), _const_spec(wdn.shape),
                     _const_spec((1, D_MODEL))]),
        out_specs=pl.BlockSpec((1, tm, D_MODEL), cur),
        out_shape=jax.ShapeDtypeStruct((B, S, D_MODEL), jnp.float32),
        scratch_shapes=[pltpu.VMEM((tm, D_FF), jnp.bfloat16)],
        compiler_params=_params(("parallel", "arbitrary")),
        name="mix_ffn",
    )(x, x, x, oa, oa, oa, ob, ob, ob, gates, gates, gates, w_out, norm2_g, wup, cw, cb, wdn, final_g)


def _prep_w_in(w):
    pad = jnp.zeros((D_MODEL, LANES - QK_ROPE), w.dtype)
    split = Q_LORA + KV_LORA + QK_ROPE
    return jnp.concatenate([w[:, :split], pad, w[:, split:]], axis=1).astype(jnp.bfloat16)


def _prep_w_q_b(w):
    w3 = w.reshape(Q_LORA, H_A, QK_NOPE + QK_ROPE)
    pad = jnp.zeros((Q_LORA, H_A, QK_PAD - QK_NOPE - QK_ROPE), w.dtype)
    return jnp.concatenate([w3, pad], axis=-1).reshape(Q_LORA, H_A * QK_PAD).astype(jnp.bfloat16)


def _prep_w_kv_b(w):
    w3 = w.reshape(KV_LORA, H_A, QK_NOPE + V_DIM)
    k = w3[:, :, :QK_NOPE].reshape(KV_LORA, H_A * QK_NOPE)
    v = w3[:, :, QK_NOPE:].reshape(KV_LORA, H_A * V_DIM)
    return jnp.concatenate([k, v], axis=1).astype(jnp.bfloat16)


def kernel(x, positions, norm1_g, w_in, q_a_norm_g, w_q_b, kv_a_norm_g, w_kv_b, rel_bias, sinks,
           w_out, norm2_g, w_up, conv_w, conv_b, w_down, final_norm_g):
    B, S, D = x.shape
    depth = norm1_g.shape[0]
    cos, sin = _rope_tables(positions)
    bias = _window_bias(rel_bias)
    f32 = jnp.float32
    for l in range(depth):
        x2d = x.reshape(B * S, D)
        qa, ka, vat, qb, kb_rep, vbt, gates = _input_projection(
            x2d, norm1_g[l][None].astype(f32), _prep_w_in(w_in[l]),
            q_a_norm_g[l][None].astype(f32), _prep_w_q_b(w_q_b[l]),
            kv_a_norm_g[l][None].astype(f32), _prep_w_kv_b(w_kv_b[l]), cos, sin, S)
        r3 = lambda a: a.reshape(B, S, a.shape[-1])
        oa = _mla_attention(r3(qa), r3(ka), vat)
        sink_rows = jnp.repeat(sinks[l].astype(f32) * LOG2E, Q_BLOCK).reshape(KV_B, 1, GROUP * Q_BLOCK)
        ob = _window_attention(r3(qb), r3(kb_rep), vbt, bias, sink_rows)
        x = _mix_ffn(x, oa, ob, r3(gates), w_out[l].astype(jnp.bfloat16), norm2_g[l][None].astype(f32),
                     w_up[l].astype(jnp.bfloat16), conv_w[l].astype(f32), conv_b[l][None].astype(f32),
                     w_down[l].astype(jnp.bfloat16), final_norm_g[None].astype(f32),
                     final_norm=(l == depth - 1))
    return x
```

```python
import functools
import math

import jax
import jax.numpy as jnp
import numpy as np
from jax import lax
from jax.experimental import pallas as pl
from jax.experimental.pallas import tpu as pltpu

D_MODEL = 1024
EPS = 1e-6
Q_BLOCK = 128
H_A = 8
QK_NOPE = 128
QK_ROPE = 64
V_DIM = 128
Q_LORA = 256
KV_LORA = 128
ROPE_THETA = 10000.0
H_B = 16
KV_B = 4
GROUP = H_B // KV_B
HD_B = 64
WINDOW = 128
NUM_BUCKETS = 32
MAX_DISTANCE = 128
D_FF = 2816
SPAN = Q_BLOCK + 2 * WINDOW

LANES = 128
QK_PAD = 256
NEG_INF = -1e30
LOG2E = math.log2(math.e)
MIX_DTYPE = jnp.bfloat16
MIN_DENOM = 2.0 ** -60
VMEM_LIMIT = 56 * 1024 * 1024

TM_PROJ = 512
TQ_MLA = 512
TK_MLA = 1024
MLA_LOOKAHEAD = 1
MLA_HEADS = 2
NQ_WIN = 4
WIN_LOOKAHEAD = 1
TM_FFN = 512
FF_CHUNK = 256
HALO = 16


def _const_spec(shape):
    nd = len(shape)
    return pl.BlockSpec(shape, lambda *_: (0,) * nd, pipeline_mode=pl.Buffered(1))


def _params(sem):
    return pltpu.CompilerParams(dimension_semantics=sem, vmem_limit_bytes=VMEM_LIMIT)


def _rms(x, g):
    return x * lax.rsqrt(jnp.mean(x * x, axis=-1, keepdims=True) + EPS) * g


def _count_unsafe(o, l):
    col = jnp.sum(o, axis=0, keepdims=True)
    bad_o = jnp.where(jnp.isfinite(col), 0.0, 1.0)
    bad_l = jnp.where(jnp.isfinite(l) & (l >= MIN_DENOM), 0.0, 1.0)
    return jnp.sum(bad_o) + jnp.sum(bad_l)


def _swap_halves_32(x):
    lane = lax.broadcasted_iota(jnp.int32, x.shape, 1)
    up = pltpu.roll(x, LANES - 32, 1)
    down = pltpu.roll(x, 32, 1)
    return jnp.where((lane & 32) == 0, up, down)


def _rope_table_kernel(pos_ref, inv_ref, sign_ref, cos_ref, sin_ref):
    ang = pos_ref[...] * inv_ref[...]
    cos_ref[...] = jnp.cos(ang)
    sin_ref[...] = jnp.sin(ang) * sign_ref[...]


def _rope_tables(positions):
    S = positions.shape[0]
    half = QK_ROPE // 2
    inv_freq = ROPE_THETA ** (-jnp.arange(half, dtype=jnp.float32) / half)
    inv = jnp.tile(inv_freq, LANES // half)[None, :]
    sign = jnp.tile(jnp.concatenate([-jnp.ones(half, jnp.float32), jnp.ones(half, jnp.float32)]),
                    LANES // QK_ROPE)[None, :]
    pos = positions.astype(jnp.float32)[:, None]
    ts = 512
    return pl.pallas_call(
        _rope_table_kernel,
        grid=(S // ts,),
        in_specs=[pl.BlockSpec((ts, 1), lambda i: (i, 0)),
                  pl.BlockSpec((1, LANES), lambda i: (0, 0)),
                  pl.BlockSpec((1, LANES), lambda i: (0, 0))],
        out_specs=[pl.BlockSpec((ts, LANES), lambda i: (i, 0)),
                   pl.BlockSpec((ts, LANES), lambda i: (i, 0))],
        out_shape=[jax.ShapeDtypeStruct((S, LANES), jnp.float32)] * 2,
        compiler_params=_params(("parallel",)),
        name="rope_tables",
    )(pos, inv, sign)


def _t5_bucket(rel):
    nb = NUM_BUCKETS // 2
    max_exact = nb // 2
    base = (rel > 0).astype(jnp.int32) * nb
    n = jnp.abs(rel)
    nf = jnp.maximum(n, 1).astype(jnp.float32)
    large = max_exact + (jnp.log(nf / max_exact) / math.log(MAX_DISTANCE / max_exact)
                         * (nb - max_exact)).astype(jnp.int32)
    large = jnp.minimum(large, nb - 1)
    return base + jnp.where(n < max_exact, n, large)


REL_SPAN = 512


def _bias_kernel(w_ref, out_ref):
    row = lax.broadcasted_iota(jnp.int32, (Q_BLOCK, REL_SPAN), 0)
    key = lax.broadcasted_iota(jnp.int32, (Q_BLOCK, Q_BLOCK), 0)
    qry = lax.broadcasted_iota(jnp.int32, (Q_BLOCK, Q_BLOCK), 1)
    for j in range(GROUP):
        x0 = jnp.broadcast_to(w_ref[j], (Q_BLOCK, REL_SPAN))
        for part in range(3):
            x = pltpu.roll(x0, (Q_BLOCK * (part + 1)) % REL_SPAN, 1)
            for bit in range(Q_BLOCK.bit_length() - 1):
                x = jnp.where((row >> bit) & 1 == 1, pltpu.roll(x, 1 << bit, 1), x)
            rel = key + (part * Q_BLOCK - WINDOW) - qry
            out_ref[0, part, :, j * Q_BLOCK:(j + 1) * Q_BLOCK] = jnp.where(
                jnp.abs(rel) <= WINDOW, x[:, 0:Q_BLOCK] * LOG2E, NEG_INF)


def _window_bias(rel_bias):
    n_rel = 2 * (Q_BLOCK + WINDOW) - 1
    offs = jnp.arange(n_rel, dtype=jnp.int32) - (n_rel // 2)
    per_off = rel_bias[_t5_bucket(offs)].astype(jnp.float32)
    per_off = jnp.pad(per_off, ((0, REL_SPAN - n_rel), (0, 0)))
    w = per_off[::-1].T.reshape(H_B, 1, REL_SPAN)
    return pl.pallas_call(
        _bias_kernel,
        grid=(KV_B,),
        in_specs=[pl.BlockSpec((GROUP, 1, REL_SPAN), lambda g: (g, 0, 0))],
        out_specs=pl.BlockSpec((1, 3, Q_BLOCK, GROUP * Q_BLOCK), lambda g: (g, 0, 0, 0)),
        out_shape=jax.ShapeDtypeStruct((KV_B, 3, Q_BLOCK, GROUP * Q_BLOCK), jnp.float32),
        compiler_params=_params(("arbitrary",)),
        name="window_bias",
    )(w)


def _proj_kernel(x_ref, g1_ref, w_in_ref, gq_ref, wq_ref, gkv_ref, wkv_ref, cos_ref, sin_ref,
                 qa_ref, ka_ref, vat_ref, qb_ref, kb_ref, vbt_ref, gate_ref, *, scale_a, scale_b):
    x = x_ref[...]
    h = _rms(x, g1_ref[...]).astype(jnp.bfloat16)
    cos = cos_ref[...]
    sin = sin_ref[...]

    def rope(blk):
        return blk * cos + _swap_halves_32(blk) * sin

    lat = jnp.dot(h, w_in_ref[:, 0:512], preferred_element_type=jnp.float32)
    q_lat = lat[:, 0:Q_LORA]
    c_kv = lat[:, Q_LORA:Q_LORA + KV_LORA]
    k_rope = rope(lat[:, 384:512]).astype(jnp.bfloat16)

    qn = _rms(q_lat, gq_ref[...]).astype(jnp.bfloat16)
    q = jnp.dot(qn, wq_ref[...], preferred_element_type=jnp.float32)
    for hd in range(H_A):
        base = hd * QK_PAD
        qa_ref[:, base:base + QK_NOPE] = (q[:, base:base + QK_NOPE] * scale_a).astype(jnp.bfloat16)
        qa_ref[:, base + QK_NOPE:base + QK_PAD] = (
            rope(q[:, base + QK_NOPE:base + QK_PAD]) * scale_a).astype(jnp.bfloat16)

    cn = _rms(c_kv, gkv_ref[...]).astype(jnp.bfloat16)
    kv = jnp.dot(cn, wkv_ref[...], preferred_element_type=jnp.float32)
    for hd in range(H_A):
        base = hd * QK_PAD
        ka_ref[:, base:base + QK_NOPE] = kv[:, hd * QK_NOPE:(hd + 1) * QK_NOPE].astype(jnp.bfloat16)
        ka_ref[:, base + QK_NOPE:base + QK_PAD] = k_rope
    vat_ref[...] = kv[:, H_A * QK_NOPE:].T.astype(jnp.bfloat16)

    qkv_b = jnp.dot(h, w_in_ref[:, 512:2048], preferred_element_type=jnp.float32)
    qb_ref[...] = (qkv_b[:, 0:1024] * scale_b).astype(jnp.bfloat16)
    lane = lax.broadcasted_iota(jnp.int32, (x.shape[0], LANES), 1)
    for g in range(KV_B):
        pair = qkv_b[:, 1024 + (g // 2) * LANES:1024 + (g // 2 + 1) * LANES]
        other = pltpu.roll(pair, HD_B, 1)
        first = (lane < HD_B) if g % 2 == 0 else (lane >= HD_B)
        rep = jnp.where(first, pair, other).astype(jnp.bfloat16)
        kb_ref[:, g * 2 * LANES:(g * 2 + 1) * LANES] = rep
        kb_ref[:, (g * 2 + 1) * LANES:(g * 2 + 2) * LANES] = rep
    vbt_ref[...] = qkv_b[:, 1280:1536].T.astype(jnp.bfloat16)

    gate_ref[...] = jnp.dot(h, w_in_ref[:, 2048:4096],
                            preferred_element_type=jnp.float32).astype(gate_ref.dtype)


def _input_projection(x2d, norm1_g, w_in_p, gq, wq_p, gkv, wkv_p, cos, sin, S):
    N = x2d.shape[0]
    tm = TM_PROJ
    s_tiles = S // tm
    row = lambda i: (i, 0)
    tab = lambda i: (i % s_tiles, 0)
    bf = jnp.bfloat16
    col = lambda i: (0, i)
    kvw = KV_B * HD_B
    outs = [((N, H_A * QK_PAD), (tm, H_A * QK_PAD), row, bf),
            ((N, H_A * QK_PAD), (tm, H_A * QK_PAD), row, bf),
            ((H_A * V_DIM, N), (H_A * V_DIM, tm), col, bf),
            ((N, H_B * HD_B), (tm, H_B * HD_B), row, bf),
            ((N, GROUP * kvw), (tm, GROUP * kvw), row, bf),
            ((kvw, N), (kvw, tm), col, bf),
            ((N, 2 * D_MODEL), (tm, 2 * D_MODEL), row, MIX_DTYPE)]
    return pl.pallas_call(
        functools.partial(_proj_kernel, scale_a=LOG2E / math.sqrt(QK_NOPE + QK_ROPE),
                          scale_b=LOG2E / math.sqrt(HD_B)),
        grid=(N // tm,),
        in_specs=[pl.BlockSpec((tm, D_MODEL), row),
                  _const_spec((1, D_MODEL)),
                  _const_spec(w_in_p.shape),
                  _const_spec((1, Q_LORA)),
                  _const_spec(wq_p.shape),
                  _const_spec((1, KV_LORA)),
                  _const_spec(wkv_p.shape),
                  pl.BlockSpec((tm, LANES), tab),
                  pl.BlockSpec((tm, LANES), tab)],
        out_specs=[pl.BlockSpec(blk, im) for _, blk, im, _ in outs],
        out_shape=[jax.ShapeDtypeStruct(shp, dt) for shp, _, _, dt in outs],
        compiler_params=_params(("parallel",)),
        name="input_projection",
    )(x2d, norm1_g, w_in_p, gq, wq_p, gkv, wkv_p, cos, sin)


def _mla_kernel(q_ref, k_ref, vt_ref, o_ref, bad_ref):
    f32 = jnp.float32
    n_chunks = k_ref.shape[1] // TK_MLA
    heads = range(MLA_HEADS)
    q = [q_ref[0, :, h * QK_PAD:(h + 1) * QK_PAD] for h in heads]


    def scores(h, c):
        kc = k_ref[0, c * TK_MLA:(c + 1) * TK_MLA, h * QK_PAD:(h + 1) * QK_PAD]
        return lax.dot_general(kc, q[h], (((1,), (1,)), ((), ())), preferred_element_type=f32)

    def pipeline(softmax, values):
        ahead = [[scores(h, c) for c in range(min(MLA_LOOKAHEAD, n_chunks))] for h in heads]
        pending = [None] * MLA_HEADS
        s = [None] * MLA_HEADS
        for c in range(n_chunks + 1):
            for h in heads:
                if c < n_chunks:
                    s[h] = ahead[h].pop(0)
                    if c + MLA_LOOKAHEAD < n_chunks:
                        ahead[h].append(scores(h, c + MLA_LOOKAHEAD))
            for h in heads:
                if pending[h] is not None:
                    values(h, c - 1, *pending[h])
                    pending[h] = None
            for h in heads:
                if c < n_chunks:
                    pending[h] = softmax(h, c, s[h])

    def first_pass():
        l = [None] * MLA_HEADS
        acc = [None] * MLA_HEADS

        def softmax(h, c, s):
            p = jnp.exp2(s)
            ps = jnp.sum(p, axis=0, keepdims=True)
            l[h] = ps if c == 0 else l[h] + ps
            return (p.astype(jnp.bfloat16),)

        def values(h, c, p):
            pv = jnp.dot(vt_ref[h * V_DIM:(h + 1) * V_DIM, c * TK_MLA:(c + 1) * TK_MLA], p,
                         preferred_element_type=f32)
            acc[h] = pv if acc[h] is None else acc[h] + pv

        pipeline(softmax, values)
        bad = 0.0
        for h in heads:
            o = acc[h] / l[h]
            bad = bad + _count_unsafe(o, l[h])
            o_ref[0, :, h * V_DIM:(h + 1) * V_DIM] = o.T.astype(o_ref.dtype)
        bad_ref[0] = bad

    first_pass()

    @pl.when(bad_ref[0] > 0.0)
    def _():
        m = [None] * MLA_HEADS
        l = [None] * MLA_HEADS
        acc = [None] * MLA_HEADS

        def softmax(h, c, s):
            mc = jnp.max(s, axis=0, keepdims=True)
            m_new = mc if c == 0 else jnp.maximum(m[h], mc)
            p = jnp.exp2(s - m_new)
            ps = jnp.sum(p, axis=0, keepdims=True)
            alpha = None if c == 0 else jnp.exp2(m[h] - m_new)
            l[h] = ps if c == 0 else alpha * l[h] + ps
            m[h] = m_new
            return p.astype(jnp.bfloat16), alpha

        def values(h, c, p, alpha):
            pv = jnp.dot(vt_ref[h * V_DIM:(h + 1) * V_DIM, c * TK_MLA:(c + 1) * TK_MLA], p,
                         preferred_element_type=f32)
            acc[h] = pv if alpha is None else alpha * acc[h] + pv

        pipeline(softmax, values)
        for h in heads:
            o_ref[0, :, h * V_DIM:(h + 1) * V_DIM] = (acc[h] / l[h]).T.astype(o_ref.dtype)


def _mla_attention(qa, ka, vat):
    B, S, _ = qa.shape
    tq = TQ_MLA
    nh = MLA_HEADS
    return pl.pallas_call(
        _mla_kernel,
        grid=(B, H_A // nh, S // tq),
        in_specs=[pl.BlockSpec((1, tq, nh * QK_PAD), lambda b, h, i: (b, i, h)),
                  pl.BlockSpec((1, S, nh * QK_PAD), lambda b, h, i: (b, 0, h)),
                  pl.BlockSpec((nh * V_DIM, S), lambda b, h, i: (h, b))],
        out_specs=pl.BlockSpec((1, tq, nh * V_DIM), lambda b, h, i: (b, i, h)),
        out_shape=jax.ShapeDtypeStruct((B, S, H_A * V_DIM), MIX_DTYPE),
        scratch_shapes=[pltpu.SMEM((1,), jnp.float32)],
        compiler_params=_params(("parallel", "parallel", "arbitrary")),
        name="mla_attention",
    )(qa, ka, vat)


def _window_kernel(q_ref, kp_ref, kc_ref, kn_ref, vp_ref, vc_ref, vn_ref, bias_ref, sink_ref, o_ref, ot_ref,
                   bad_ref, *, n_steps):
    step = pl.program_id(1)
    gw = GROUP * HD_B
    lane = lax.broadcasted_iota(jnp.int32, (Q_BLOCK, gw), 1)
    work = [(i, g) for i in range(NQ_WIN) for g in range(KV_B)]

    def key_blocks(i):
        blocks = []
        for blk in (i - 1, i, i + 1):
            if blk < 0:
                blocks.append(("prev", 0))
            elif blk >= NQ_WIN:
                blocks.append(("next", 0))
            else:
                blocks.append(("cur", blk))
        return blocks

    def k_slab(i, g):
        refs = {"prev": kp_ref, "cur": kc_ref, "next": kn_ref}
        return jnp.concatenate([refs[kind][0, o * Q_BLOCK:(o + 1) * Q_BLOCK, g * gw:(g + 1) * gw]
                                for kind, o in key_blocks(i)], axis=0)

    def vt_slab(i, g):
        refs = {"prev": vp_ref, "cur": vc_ref, "next": vn_ref}
        return jnp.concatenate([refs[kind][g * HD_B:(g + 1) * HD_B, o * Q_BLOCK:(o + 1) * Q_BLOCK]
                                for kind, o in key_blocks(i)], axis=1)

    def scores(t):
        i, g = work[t]
        qg = q_ref[0, i * Q_BLOCK:(i + 1) * Q_BLOCK, g * gw:(g + 1) * gw]
        wt = jnp.concatenate(
            [jnp.where((lane >= j * HD_B) & (lane < (j + 1) * HD_B), qg, jnp.zeros_like(qg))
             for j in range(GROUP)], axis=0)
        return lax.dot_general(k_slab(i, g), wt, (((1,), (1,)), ((), ())), preferred_element_type=jnp.float32)

    def softmax(t, s, subtract_max):
        i, g = work[t]
        parts = []
        for part in range(3):
            sp = s[part * Q_BLOCK:(part + 1) * Q_BLOCK] + bias_ref[g, part]
            if part == 0 and i == 0:
                sp = jnp.where(step > 0, sp, NEG_INF)
            if part == 2 and i == NQ_WIN - 1:
                sp = jnp.where(step < n_steps - 1, sp, NEG_INF)
            parts.append(sp)
        s = jnp.concatenate(parts, axis=0)
        sink = sink_ref[g]
        if subtract_max:
            m = jnp.maximum(jnp.max(s, axis=0, keepdims=True), sink)
            s, sink = s - m, sink - m
        p = jnp.exp2(s)
        l = jnp.sum(p, axis=0, keepdims=True) + jnp.exp2(sink)
        return p.astype(jnp.bfloat16), l

    def values(t, p, l):
        i, g = work[t]
        r = jnp.dot(vt_slab(i, g), p, preferred_element_type=jnp.float32) / l
        for j in range(GROUP):
            hd = g * GROUP + j
            ot_ref[hd * HD_B:(hd + 1) * HD_B, i * Q_BLOCK:(i + 1) * Q_BLOCK] = r[:, j * Q_BLOCK:(j + 1) * Q_BLOCK]
        if g == KV_B - 1:
            o_ref[0, i * Q_BLOCK:(i + 1) * Q_BLOCK, :] = (
                ot_ref[:, i * Q_BLOCK:(i + 1) * Q_BLOCK].T.astype(o_ref.dtype))
        return _count_unsafe(r, l)

    def pipeline(subtract_max):
        ahead = [scores(t) for t in range(WIN_LOOKAHEAD)]
        pending = None
        bad = 0.0
        for t in range(len(work) + 1):
            if t < len(work):
                s = ahead.pop(0)
                if t + WIN_LOOKAHEAD < len(work):
                    ahead.append(scores(t + WIN_LOOKAHEAD))
            if pending is not None:
                bad = bad + values(*pending)
            if t < len(work):
                pending = (t,) + softmax(t, s, subtract_max)
        return bad

    bad_ref[0] = pipeline(subtract_max=False)

    @pl.when(bad_ref[0] > 0.0)
    def _():
        pipeline(subtract_max=True)


def _window_attention(qb, kb_rep, vbt, bias, sink_rows):
    B, S, _ = qb.shape
    nblk = S // Q_BLOCK
    rows = NQ_WIN * Q_BLOCK
    n_steps = S // rows
    w = H_B * HD_B
    kprev = lambda b, s: (b, jnp.maximum(s * NQ_WIN - 1, 0), 0)
    cur = lambda b, s: (b, s, 0)
    knext = lambda b, s: (b, jnp.minimum((s + 1) * NQ_WIN, nblk - 1), 0)
    vprev = lambda b, s: (0, b * nblk + jnp.maximum(s * NQ_WIN - 1, 0))
    vcur = lambda b, s: (0, b * n_steps + s)
    vnext = lambda b, s: (0, b * nblk + jnp.minimum((s + 1) * NQ_WIN, nblk - 1))
    kvw = KV_B * HD_B
    return pl.pallas_call(
        functools.partial(_window_kernel, n_steps=n_steps),
        grid=(B, n_steps),
        in_specs=[pl.BlockSpec((1, rows, w), cur),
                  pl.BlockSpec((1, Q_BLOCK, w), kprev),
                  pl.BlockSpec((1, rows, w), cur),
                  pl.BlockSpec((1, Q_BLOCK, w), knext),
                  pl.BlockSpec((kvw, Q_BLOCK), vprev),
                  pl.BlockSpec((kvw, rows), vcur),
                  pl.BlockSpec((kvw, Q_BLOCK), vnext),
                  _const_spec(bias.shape),
                  _const_spec(sink_rows.shape)],
        out_specs=pl.BlockSpec((1, rows, w), cur),
        out_shape=jax.ShapeDtypeStruct((B, S, w), MIX_DTYPE),
        scratch_shapes=[pltpu.VMEM((w, rows), jnp.float32), pltpu.SMEM((1,), jnp.float32)],
        compiler_params=_params(("parallel", "arbitrary")),
        name="window_attention",
    )(qb, kb_rep, kb_rep, kb_rep, vbt, vbt, vbt, bias, sink_rows)


def _mix_ffn_kernel(xp_ref, xc_ref, xn_ref, ap_ref, ac_ref, an_ref, bp_ref, bc_ref, bn_ref,
                    gp_ref, gc_ref, gn_ref, wout_ref, g2_ref, wup_ref, cw_ref, cb_ref, wdn_ref, gf_ref,
                    o_ref, act_ref, *, n_tiles, final_norm):
    i = pl.program_id(1)
    tm = xc_ref.shape[1]
    f32 = jnp.float32

    def ext(p_ref, c_ref, n_ref):
        return jnp.concatenate([p_ref[0], c_ref[0], n_ref[0]], axis=0)

    gates = ext(gp_ref, gc_ref, gn_ref)
    mixed = (jax.nn.sigmoid(gates[:, 0:D_MODEL].astype(f32)) * ext(ap_ref, ac_ref, an_ref).astype(f32)
             + jax.nn.sigmoid(gates[:, D_MODEL:2 * D_MODEL].astype(f32)) * ext(bp_ref, bc_ref, bn_ref).astype(f32))
    x1e = ext(xp_ref, xc_ref, xn_ref) + jnp.dot(mixed.astype(jnp.bfloat16), wout_ref[...],
                                                preferred_element_type=f32)
    x1 = x1e[HALO:HALO + tm]
    h2e = _rms(x1e, g2_ref[...]).astype(jnp.bfloat16)
    hp = jnp.where(i > 0, h2e[0:HALO], jnp.zeros((HALO, D_MODEL), jnp.bfloat16))
    hn = jnp.where(i < n_tiles - 1, h2e[HALO + tm:], jnp.zeros((HALO, D_MODEL), jnp.bfloat16))
    he = jnp.concatenate([hp, h2e[HALO:HALO + tm], hn], axis=0)
    n_chunks = D_FF // FF_CHUNK

    def cols(ref, j):
        g0 = j * FF_CHUNK
        return jnp.concatenate([ref[:, g0:g0 + FF_CHUNK], ref[:, D_FF + g0:D_FF + g0 + FF_CHUNK]], axis=1)

    def up(j):
        return jnp.dot(he, cols(wup_ref, j), preferred_element_type=jnp.float32)

    u_next = up(0)
    for j in range(n_chunks):
        u = u_next
        if j + 1 < n_chunks:
            u_next = up(j + 1)
        cw = cols(cw_ref, j)
        cb = cols(cb_ref, j)
        y = (u[HALO - 1:HALO - 1 + tm] * cw[0:1] + u[HALO:HALO + tm] * cw[1:2]
             + u[HALO + 1:HALO + 1 + tm] * cw[2:3] + cb)
        gate = y[:, 0:FF_CHUNK]
        val = y[:, FF_CHUNK:2 * FF_CHUNK]
        act_ref[:, j * FF_CHUNK:(j + 1) * FF_CHUNK] = (jax.nn.silu(gate) * val).astype(jnp.bfloat16)
    x2 = x1 + jnp.dot(act_ref[...], wdn_ref[...], preferred_element_type=jnp.float32)
    o_ref[0] = _rms(x2, gf_ref[...]) if final_norm else x2


def _mix_ffn(x, oa, ob, gates, w_out, norm2_g, wup, cw, cb, wdn, final_g, final_norm):
    B, S, _ = x.shape
    tm = TM_FFN
    n_tiles = S // tm
    hb = tm // HALO
    cur = lambda b, i: (b, i, 0)
    prev = lambda b, i: (b, jnp.maximum(i * hb - 1, 0), 0)
    nxt = lambda b, i: (b, jnp.minimum((i + 1) * hb, S // HALO - 1), 0)

    def halo_specs(width):
        return [pl.BlockSpec((1, HALO, width), prev), pl.BlockSpec((1, tm, width), cur),
                pl.BlockSpec((1, HALO, width), nxt)]

    return pl.pallas_call(
        functools.partial(_mix_ffn_kernel, n_tiles=n_tiles, final_norm=final_norm),
        grid=(B, n_tiles),
        in_specs=(halo_specs(D_MODEL) + halo_specs(D_MODEL) + halo_specs(D_MODEL) + halo_specs(2 * D_MODEL)
                  + [_const_spec(w_out.shape), _const_spec((1, D_MODEL)), _const_spec(wup.shape),
                     _const_spec(cw.shape), _const_spec(cb.shape), _const_spec(wdn.shape),
                     _const_spec((1, D_MODEL))]),
        out_specs=pl.BlockSpec((1, tm, D_MODEL), cur),
        out_shape=jax.ShapeDtypeStruct((B, S, D_MODEL), jnp.float32),
        scratch_shapes=[pltpu.VMEM((tm, D_FF), jnp.bfloat16)],
        compiler_params=_params(("parallel", "arbitrary")),
        name="mix_ffn",
    )(x, x, x, oa, oa, oa, ob, ob, ob, gates, gates, gates, w_out, norm2_g, wup, cw, cb, wdn, final_g)


def _prep_w_in(w):
    pad = jnp.zeros((D_MODEL, LANES - QK_ROPE), w.dtype)
    split = Q_LORA + KV_LORA + QK_ROPE
    return jnp.concatenate([w[:, :split], pad, w[:, split:]], axis=1).astype(jnp.bfloat16)


def _prep_w_q_b(w):
    w3 = w.reshape(Q_LORA, H_A, QK_NOPE + QK_ROPE)
    pad = jnp.zeros((Q_LORA, H_A, QK_PAD - QK_NOPE - QK_ROPE), w.dtype)
    return jnp.concatenate([w3, pad], axis=-1).reshape(Q_LORA, H_A * QK_PAD).astype(jnp.bfloat16)


def _prep_w_kv_b(w):
    w3 = w.reshape(KV_LORA, H_A, QK_NOPE + V_DIM)
    k = w3[:, :, :QK_NOPE].reshape(KV_LORA, H_A * QK_NOPE)
    v = w3[:, :, QK_NOPE:].reshape(KV_LORA, H_A * V_DIM)
    return jnp.concatenate([k, v], axis=1).astype(jnp.bfloat16)


def kernel(x, positions, norm1_g, w_in, q_a_norm_g, w_q_b, kv_a_norm_g, w_kv_b, rel_bias, sinks,
           w_out, norm2_g, w_up, conv_w, conv_b, w_down, final_norm_g):
    B, S, D = x.shape
    depth = norm1_g.shape[0]
    cos, sin = _rope_tables(positions)
    bias = _window_bias(rel_bias)
    f32 = jnp.float32
    for l in range(depth):
        x2d = x.reshape(B * S, D)
        qa, ka, vat, qb, kb_rep, vbt, gates = _input_projection(
            x2d, norm1_g[l][None].astype(f32), _prep_w_in(w_in[l]),
            q_a_norm_g[l][None].astype(f32), _prep_w_q_b(w_q_b[l]),
            kv_a_norm_g[l][None].astype(f32), _prep_w_kv_b(w_kv_b[l]), cos, sin, S)
        r3 = lambda a: a.reshape(B, S, a.shape[-1])
        oa = _mla_attention(r3(qa), r3(ka), vat)
        sink_rows = jnp.repeat(sinks[l].astype(f32) * LOG2E, Q_BLOCK).reshape(KV_B, 1, GROUP * Q_BLOCK)
        ob = _window_attention(r3(qb), r3(kb_rep), vbt, bias, sink_rows)
        x = _mix_ffn(x, oa, ob, r3(gates), w_out[l].astype(jnp.bfloat16), norm2_g[l][None].astype(f32),
                     w_up[l].astype(jnp.bfloat16), conv_w[l].astype(f32), conv_b[l][None].astype(f32),
                     w_down[l].astype(jnp.bfloat16), final_norm_g[None].astype(f32),
                     final_norm=(l == depth - 1))
    return x
```

```python
import functools
import math

import jax
import jax.numpy as jnp
import numpy as np
from jax import lax
from jax.experimental import pallas as pl
from jax.experimental.pallas import tpu as pltpu

D_MODEL = 1024
EPS = 1e-6
Q_BLOCK = 128
H_A = 8
QK_NOPE = 128
QK_ROPE = 64
V_DIM = 128
Q_LORA = 256
KV_LORA = 128
ROPE_THETA = 10000.0
H_B = 16
KV_B = 4
GROUP = H_B // KV_B
HD_B = 64
WINDOW = 128
NUM_BUCKETS = 32
MAX_DISTANCE = 128
D_FF = 2816
SPAN = Q_BLOCK + 2 * WINDOW

LANES = 128
QK_PAD = 256
NEG_INF = -1e30
LOG2E = math.log2(math.e)
MIX_DTYPE = jnp.bfloat16
MIN_DENOM = 2.0 ** -60
VMEM_LIMIT = 56 * 1024 * 1024

TM_PROJ = 512
TQ_MLA = 512
TK_MLA = 1024
MLA_LOOKAHEAD = 1
MLA_HEADS = 2
NQ_WIN = 8
WIN_LOOKAHEAD = 1
TM_FFN = 512
FF_CHUNK = 256
HALO = 8
BF16_ROWS = 16


def _const_spec(shape):
    nd = len(shape)
    return pl.BlockSpec(shape, lambda *_: (0,) * nd, pipeline_mode=pl.Buffered(1))


def _params(sem):
    return pltpu.CompilerParams(dimension_semantics=sem, vmem_limit_bytes=VMEM_LIMIT)


def _rms(x, g):
    return x * lax.rsqrt(jnp.mean(x * x, axis=-1, keepdims=True) + EPS) * g


def _count_unsafe(o, l):
    col = jnp.sum(o, axis=0, keepdims=True)
    bad_o = jnp.where(jnp.isfinite(col), 0.0, 1.0)
    bad_l = jnp.where(jnp.isfinite(l) & (l >= MIN_DENOM), 0.0, 1.0)
    return jnp.sum(bad_o) + jnp.sum(bad_l)


def _swap_halves_32(x):
    lane = lax.broadcasted_iota(jnp.int32, x.shape, 1)
    up = pltpu.roll(x, LANES - 32, 1)
    down = pltpu.roll(x, 32, 1)
    return jnp.where((lane & 32) == 0, up, down)


def _rope_table_kernel(pos_ref, inv_ref, sign_ref, cos_ref, sin_ref):
    ang = pos_ref[...] * inv_ref[...]
    cos_ref[...] = jnp.cos(ang)
    sin_ref[...] = jnp.sin(ang) * sign_ref[...]


def _rope_tables(positions):
    S = positions.shape[0]
    half = QK_ROPE // 2
    inv_freq = ROPE_THETA ** (-jnp.arange(half, dtype=jnp.float32) / half)
    inv = jnp.tile(inv_freq, LANES // half)[None, :]
    sign = jnp.tile(jnp.concatenate([-jnp.ones(half, jnp.float32), jnp.ones(half, jnp.float32)]),
                    LANES // QK_ROPE)[None, :]
    pos = positions.astype(jnp.float32)[:, None]
    ts = 512
    return pl.pallas_call(
        _rope_table_kernel,
        grid=(S // ts,),
        in_specs=[pl.BlockSpec((ts, 1), lambda i: (i, 0)),
                  pl.BlockSpec((1, LANES), lambda i: (0, 0)),
                  pl.BlockSpec((1, LANES), lambda i: (0, 0))],
        out_specs=[pl.BlockSpec((ts, LANES), lambda i: (i, 0)),
                   pl.BlockSpec((ts, LANES), lambda i: (i, 0))],
        out_shape=[jax.ShapeDtypeStruct((S, LANES), jnp.float32)] * 2,
        compiler_params=_params(("parallel",)),
        name="rope_tables",
    )(pos, inv, sign)


def _t5_bucket(rel):
    nb = NUM_BUCKETS // 2
    max_exact = nb // 2
    base = (rel > 0).astype(jnp.int32) * nb
    n = jnp.abs(rel)
    nf = jnp.maximum(n, 1).astype(jnp.float32)
    large = max_exact + (jnp.log(nf / max_exact) / math.log(MAX_DISTANCE / max_exact)
                         * (nb - max_exact)).astype(jnp.int32)
    large = jnp.minimum(large, nb - 1)
    return base + jnp.where(n < max_exact, n, large)


REL_SPAN = 512


def _bias_kernel(w_ref, out_ref):
    row = lax.broadcasted_iota(jnp.int32, (Q_BLOCK, REL_SPAN), 0)
    key = lax.broadcasted_iota(jnp.int32, (Q_BLOCK, Q_BLOCK), 0)
    qry = lax.broadcasted_iota(jnp.int32, (Q_BLOCK, Q_BLOCK), 1)
    for j in range(GROUP):
        x0 = jnp.broadcast_to(w_ref[j], (Q_BLOCK, REL_SPAN))
        for part in range(3):
            x = pltpu.roll(x0, (Q_BLOCK * (part + 1)) % REL_SPAN, 1)
            for bit in range(Q_BLOCK.bit_length() - 1):
                x = jnp.where((row >> bit) & 1 == 1, pltpu.roll(x, 1 << bit, 1), x)
            rel = key + (part * Q_BLOCK - WINDOW) - qry
            out_ref[0, part, :, j * Q_BLOCK:(j + 1) * Q_BLOCK] = jnp.where(
                jnp.abs(rel) <= WINDOW, x[:, 0:Q_BLOCK] * LOG2E, NEG_INF)


def _window_bias(rel_bias):
    n_rel = 2 * (Q_BLOCK + WINDOW) - 1
    offs = jnp.arange(n_rel, dtype=jnp.int32) - (n_rel // 2)
    per_off = rel_bias[_t5_bucket(offs)].astype(jnp.float32)
    per_off = jnp.pad(per_off, ((0, REL_SPAN - n_rel), (0, 0)))
    w = per_off[::-1].T.reshape(H_B, 1, REL_SPAN)
    return pl.pallas_call(
        _bias_kernel,
        grid=(KV_B,),
        in_specs=[pl.BlockSpec((GROUP, 1, REL_SPAN), lambda g: (g, 0, 0))],
        out_specs=pl.BlockSpec((1, 3, Q_BLOCK, GROUP * Q_BLOCK), lambda g: (g, 0, 0, 0)),
        out_shape=jax.ShapeDtypeStruct((KV_B, 3, Q_BLOCK, GROUP * Q_BLOCK), jnp.float32),
        compiler_params=_params(("arbitrary",)),
        name="window_bias",
    )(w)


def _proj_kernel(x_ref, g1_ref, w_in_ref, gq_ref, wq_ref, gkv_ref, wkv_ref, cos_ref, sin_ref,
                 qa_ref, ka_ref, vat_ref, qb_ref, kb_ref, vbt_ref, gate_ref, *, scale_a, scale_b):
    x = x_ref[...]
    h = _rms(x, g1_ref[...]).astype(jnp.bfloat16)
    cos = cos_ref[...]
    sin = sin_ref[...]

    def rope(blk):
        return blk * cos + _swap_halves_32(blk) * sin

    lat = jnp.dot(h, w_in_ref[:, 0:512], preferred_element_type=jnp.float32)
    q_lat = lat[:, 0:Q_LORA]
    c_kv = lat[:, Q_LORA:Q_LORA + KV_LORA]
    k_rope = rope(lat[:, 384:512]).astype(jnp.bfloat16)

    qn = _rms(q_lat, gq_ref[...]).astype(jnp.bfloat16)
    q = jnp.dot(qn, wq_ref[...], preferred_element_type=jnp.float32)
    for hd in range(H_A):
        base = hd * QK_PAD
        qa_ref[:, base:base + QK_NOPE] = (q[:, base:base + QK_NOPE] * scale_a).astype(jnp.bfloat16)
        qa_ref[:, base + QK_NOPE:base + QK_PAD] = (
            rope(q[:, base + QK_NOPE:base + QK_PAD]) * scale_a).astype(jnp.bfloat16)

    cn = _rms(c_kv, gkv_ref[...]).astype(jnp.bfloat16)
    kv = jnp.dot(cn, wkv_ref[...], preferred_element_type=jnp.float32)
    for hd in range(H_A):
        base = hd * QK_PAD
        ka_ref[:, base:base + QK_NOPE] = kv[:, hd * QK_NOPE:(hd + 1) * QK_NOPE].astype(jnp.bfloat16)
        ka_ref[:, base + QK_NOPE:base + QK_PAD] = k_rope
    vat_ref[...] = kv[:, H_A * QK_NOPE:].T.astype(jnp.bfloat16)

    qkv_b = jnp.dot(h, w_in_ref[:, 512:2048], preferred_element_type=jnp.float32)
    qb_ref[...] = (qkv_b[:, 0:1024] * scale_b).astype(jnp.bfloat16)
    lane = lax.broadcasted_iota(jnp.int32, (x.shape[0], LANES), 1)
    for g in range(KV_B):
        pair = qkv_b[:, 1024 + (g // 2) * LANES:1024 + (g // 2 + 1) * LANES]
        other = pltpu.roll(pair, HD_B, 1)
        first = (lane < HD_B) if g % 2 == 0 else (lane >= HD_B)
        rep = jnp.where(first, pair, other).astype(jnp.bfloat16)
        kb_ref[:, g * 2 * LANES:(g * 2 + 1) * LANES] = rep
        kb_ref[:, (g * 2 + 1) * LANES:(g * 2 + 2) * LANES] = rep
    vbt_ref[...] = qkv_b[:, 1280:1536].T.astype(jnp.bfloat16)

    gate_ref[...] = jnp.dot(h, w_in_ref[:, 2048:4096],
                            preferred_element_type=jnp.float32).astype(gate_ref.dtype)


def _input_projection(x2d, norm1_g, w_in_p, gq, wq_p, gkv, wkv_p, cos, sin, S):
    N = x2d.shape[0]
    tm = TM_PROJ
    s_tiles = S // tm
    row = lambda i: (i, 0)
    tab = lambda i: (i % s_tiles, 0)
    bf = jnp.bfloat16
    col = lambda i: (0, i)
    kvw = KV_B * HD_B
    outs = [((N, H_A * QK_PAD), (tm, H_A * QK_PAD), row, bf),
            ((N, H_A * QK_PAD), (tm, H_A * QK_PAD), row, bf),
            ((H_A * V_DIM, N), (H_A * V_DIM, tm), col, bf),
            ((N, H_B * HD_B), (tm, H_B * HD_B), row, bf),
            ((N, GROUP * kvw), (tm, GROUP * kvw), row, bf),
            ((kvw, N), (kvw, tm), col, bf),
            ((N, 2 * D_MODEL), (tm, 2 * D_MODEL), row, MIX_DTYPE)]
    return pl.pallas_call(
        functools.partial(_proj_kernel, scale_a=LOG2E / math.sqrt(QK_NOPE + QK_ROPE),
                          scale_b=LOG2E / math.sqrt(HD_B)),
        grid=(N // tm,),
        in_specs=[pl.BlockSpec((tm, D_MODEL), row),
                  _const_spec((1, D_MODEL)),
                  _const_spec(w_in_p.shape),
                  _const_spec((1, Q_LORA)),
                  _const_spec(wq_p.shape),
                  _const_spec((1, KV_LORA)),
                  _const_spec(wkv_p.shape),
                  pl.BlockSpec((tm, LANES), tab),
                  pl.BlockSpec((tm, LANES), tab)],
        out_specs=[pl.BlockSpec(blk, im) for _, blk, im, _ in outs],
        out_shape=[jax.ShapeDtypeStruct(shp, dt) for shp, _, _, dt in outs],
        compiler_params=_params(("parallel",)),
        name="input_projection",
    )(x2d, norm1_g, w_in_p, gq, wq_p, gkv, wkv_p, cos, sin)


def _mla_kernel(q_ref, k_ref, vt_ref, o_ref, bad_ref):
    f32 = jnp.float32
    n_chunks = k_ref.shape[1] // TK_MLA
    heads = range(MLA_HEADS)
    q = [q_ref[0, :, h * QK_PAD:(h + 1) * QK_PAD] for h in heads]


    def scores(h, c):
        kc = k_ref[0, c * TK_MLA:(c + 1) * TK_MLA, h * QK_PAD:(h + 1) * QK_PAD]
        return lax.dot_general(kc, q[h], (((1,), (1,)), ((), ())), preferred_element_type=f32)

    def pipeline(softmax, values):
        ahead = [[scores(h, c) for c in range(min(MLA_LOOKAHEAD, n_chunks))] for h in heads]
        pending = [None] * MLA_HEADS
        s = [None] * MLA_HEADS
        for c in range(n_chunks + 1):
            for h in heads:
                if c < n_chunks:
                    s[h] = ahead[h].pop(0)
                    if c + MLA_LOOKAHEAD < n_chunks:
                        ahead[h].append(scores(h, c + MLA_LOOKAHEAD))
            for h in heads:
                if pending[h] is not None:
                    values(h, c - 1, *pending[h])
                    pending[h] = None
            for h in heads:
                if c < n_chunks:
                    pending[h] = softmax(h, c, s[h])

    def first_pass():
        l = [None] * MLA_HEADS
        acc = [None] * MLA_HEADS

        def softmax(h, c, s):
            p = jnp.exp2(s)
            ps = jnp.sum(p, axis=0, keepdims=True)
            l[h] = ps if c == 0 else l[h] + ps
            return (p.astype(jnp.bfloat16),)

        def values(h, c, p):
            pv = jnp.dot(vt_ref[h * V_DIM:(h + 1) * V_DIM, c * TK_MLA:(c + 1) * TK_MLA], p,
                         preferred_element_type=f32)
            acc[h] = pv if acc[h] is None else acc[h] + pv

        pipeline(softmax, values)
        bad = 0.0
        for h in heads:
            o = acc[h] / l[h]
            bad = bad + _count_unsafe(o, l[h])
            o_ref[0, :, h * V_DIM:(h + 1) * V_DIM] = o.T.astype(o_ref.dtype)
        bad_ref[0] = bad

    first_pass()

    @pl.when(bad_ref[0] > 0.0)
    def _():
        m = [None] * MLA_HEADS
        l = [None] * MLA_HEADS
        acc = [None] * MLA_HEADS

        def softmax(h, c, s):
            mc = jnp.max(s, axis=0, keepdims=True)
            m_new = mc if c == 0 else jnp.maximum(m[h], mc)
            p = jnp.exp2(s - m_new)
            ps = jnp.sum(p, axis=0, keepdims=True)
            alpha = None if c == 0 else jnp.exp2(m[h] - m_new)
            l[h] = ps if c == 0 else alpha * l[h] + ps
            m[h] = m_new
            return p.astype(jnp.bfloat16), alpha

        def values(h, c, p, alpha):
            pv = jnp.dot(vt_ref[h * V_DIM:(h + 1) * V_DIM, c * TK_MLA:(c + 1) * TK_MLA], p,
                         preferred_element_type=f32)
            acc[h] = pv if alpha is None else alpha * acc[h] + pv

        pipeline(softmax, values)
        for h in heads:
            o_ref[0, :, h * V_DIM:(h + 1) * V_DIM] = (acc[h] / l[h]).T.astype(o_ref.dtype)


def _mla_attention(qa, ka, vat):
    B, S, _ = qa.shape
    tq = TQ_MLA
    nh = MLA_HEADS
    return pl.pallas_call(
        _mla_kernel,
        grid=(B, H_A // nh, S // tq),
        in_specs=[pl.BlockSpec((1, tq, nh * QK_PAD), lambda b, h, i: (b, i, h)),
                  pl.BlockSpec((1, S, nh * QK_PAD), lambda b, h, i: (b, 0, h)),
                  pl.BlockSpec((nh * V_DIM, S), lambda b, h, i: (h, b))],
        out_specs=pl.BlockSpec((1, tq, nh * V_DIM), lambda b, h, i: (b, i, h)),
        out_shape=jax.ShapeDtypeStruct((B, S, H_A * V_DIM), MIX_DTYPE),
        scratch_shapes=[pltpu.SMEM((1,), jnp.float32)],
        compiler_params=_params(("parallel", "parallel", "arbitrary")),
        name="mla_attention",
    )(qa, ka, vat)


def _window_kernel(q_ref, kp_ref, kc_ref, kn_ref, vp_ref, vc_ref, vn_ref, bias_ref, sink_ref, o_ref, ot_ref,
                   bad_ref, *, n_steps):
    step = pl.program_id(1)
    gw = GROUP * HD_B
    lane = lax.broadcasted_iota(jnp.int32, (Q_BLOCK, gw), 1)
    work = [(i, g) for i in range(NQ_WIN) for g in range(KV_B)]

    def key_blocks(i):
        blocks = []
        for blk in (i - 1, i, i + 1):
            if blk < 0:
                blocks.append(("prev", 0))
            elif blk >= NQ_WIN:
                blocks.append(("next", 0))
            else:
                blocks.append(("cur", blk))
        return blocks

    def k_slab(i, g):
        refs = {"prev": kp_ref, "cur": kc_ref, "next": kn_ref}
        return jnp.concatenate([refs[kind][0, o * Q_BLOCK:(o + 1) * Q_BLOCK, g * gw:(g + 1) * gw]
                                for kind, o in key_blocks(i)], axis=0)

    def vt_slab(i, g):
        refs = {"prev": vp_ref, "cur": vc_ref, "next": vn_ref}
        return jnp.concatenate([refs[kind][g * HD_B:(g + 1) * HD_B, o * Q_BLOCK:(o + 1) * Q_BLOCK]
                                for kind, o in key_blocks(i)], axis=1)

    def scores(t):
        i, g = work[t]
        qg = q_ref[0, i * Q_BLOCK:(i + 1) * Q_BLOCK, g * gw:(g + 1) * gw]
        wt = jnp.concatenate(
            [jnp.where((lane >= j * HD_B) & (lane < (j + 1) * HD_B), qg, jnp.zeros_like(qg))
             for j in range(GROUP)], axis=0)
        return lax.dot_general(k_slab(i, g), wt, (((1,), (1,)), ((), ())), preferred_element_type=jnp.float32)

    def softmax(t, s, subtract_max):
        i, g = work[t]
        parts = []
        for part in range(3):
            sp = s[part * Q_BLOCK:(part + 1) * Q_BLOCK] + bias_ref[g, part]
            if part == 0 and i == 0:
                sp = jnp.where(step > 0, sp, NEG_INF)
            if part == 2 and i == NQ_WIN - 1:
                sp = jnp.where(step < n_steps - 1, sp, NEG_INF)
            parts.append(sp)
        s = jnp.concatenate(parts, axis=0)
        sink = sink_ref[g]
        if subtract_max:
            m = jnp.maximum(jnp.max(s, axis=0, keepdims=True), sink)
            s, sink = s - m, sink - m
        p = jnp.exp2(s)
        l = jnp.sum(p, axis=0, keepdims=True) + jnp.exp2(sink)
        return p.astype(jnp.bfloat16), l

    def values(t, p, l):
        i, g = work[t]
        r = jnp.dot(vt_slab(i, g), p, preferred_element_type=jnp.float32) / l
        for j in range(GROUP):
            hd = g * GROUP + j
            ot_ref[hd * HD_B:(hd + 1) * HD_B, i * Q_BLOCK:(i + 1) * Q_BLOCK] = r[:, j * Q_BLOCK:(j + 1) * Q_BLOCK]
        if g == KV_B - 1:
            o_ref[0, i * Q_BLOCK:(i + 1) * Q_BLOCK, :] = (
                ot_ref[:, i * Q_BLOCK:(i + 1) * Q_BLOCK].T.astype(o_ref.dtype))
        return _count_unsafe(r, l)

    def pipeline(subtract_max):
        ahead = [scores(t) for t in range(WIN_LOOKAHEAD)]
        pending = None
        bad = 0.0
        for t in range(len(work) + 1):
            if t < len(work):
                s = ahead.pop(0)
                if t + WIN_LOOKAHEAD < len(work):
                    ahead.append(scores(t + WIN_LOOKAHEAD))
            if pending is not None:
                bad = bad + values(*pending)
            if t < len(work):
                pending = (t,) + softmax(t, s, subtract_max)
        return bad

    bad_ref[0] = pipeline(subtract_max=False)

    @pl.when(bad_ref[0] > 0.0)
    def _():
        pipeline(subtract_max=True)


def _window_attention(qb, kb_rep, vbt, bias, sink_rows):
    B, S, _ = qb.shape
    nblk = S // Q_BLOCK
    rows = NQ_WIN * Q_BLOCK
    n_steps = S // rows
    w = H_B * HD_B
    kprev = lambda b, s: (b, jnp.maximum(s * NQ_WIN - 1, 0), 0)
    cur = lambda b, s: (b, s, 0)
    knext = lambda b, s: (b, jnp.minimum((s + 1) * NQ_WIN, nblk - 1), 0)
    vprev = lambda b, s: (0, b * nblk + jnp.maximum(s * NQ_WIN - 1, 0))
    vcur = lambda b, s: (0, b * n_steps + s)
    vnext = lambda b, s: (0, b * nblk + jnp.minimum((s + 1) * NQ_WIN, nblk - 1))
    kvw = KV_B * HD_B
    return pl.pallas_call(
        functools.partial(_window_kernel, n_steps=n_steps),
        grid=(B, n_steps),
        in_specs=[pl.BlockSpec((1, rows, w), cur),
                  pl.BlockSpec((1, Q_BLOCK, w), kprev),
                  pl.BlockSpec((1, rows, w), cur),
                  pl.BlockSpec((1, Q_BLOCK, w), knext),
                  pl.BlockSpec((kvw, Q_BLOCK), vprev),
                  pl.BlockSpec((kvw, rows), vcur),
                  pl.BlockSpec((kvw, Q_BLOCK), vnext),
                  _const_spec(bias.shape),
                  _const_spec(sink_rows.shape)],
        out_specs=pl.BlockSpec((1, rows, w), cur),
        out_shape=jax.ShapeDtypeStruct((B, S, w), MIX_DTYPE),
        scratch_shapes=[pltpu.VMEM((w, rows), jnp.float32), pltpu.SMEM((1,), jnp.float32)],
        compiler_params=_params(("parallel", "arbitrary")),
        name="window_attention",
    )(qb, kb_rep, kb_rep, kb_rep, vbt, vbt, vbt, bias, sink_rows)


def _mix_ffn_kernel(xp_ref, xc_ref, xn_ref, ap_ref, ac_ref, an_ref, bp_ref, bc_ref, bn_ref,
                    gp_ref, gc_ref, gn_ref, wout_ref, g2_ref, wup_ref, cw_ref, cb_ref, wdn_ref, gf_ref,
                    o_ref, act_ref, *, n_tiles, final_norm):
    i = pl.program_id(1)
    tm = xc_ref.shape[1]
    f32 = jnp.float32

    def ext(p_ref, c_ref, n_ref):
        p = p_ref[0].astype(f32)
        return jnp.concatenate([p[p.shape[0] - HALO:], c_ref[0].astype(f32), n_ref[0].astype(f32)[0:HALO]], axis=0)

    gates = ext(gp_ref, gc_ref, gn_ref)
    mixed = (jax.nn.sigmoid(gates[:, 0:D_MODEL]) * ext(ap_ref, ac_ref, an_ref)
             + jax.nn.sigmoid(gates[:, D_MODEL:2 * D_MODEL]) * ext(bp_ref, bc_ref, bn_ref))
    x1e = ext(xp_ref, xc_ref, xn_ref) + jnp.dot(mixed.astype(jnp.bfloat16), wout_ref[...],
                                                preferred_element_type=f32)
    x1 = x1e[HALO:HALO + tm]
    h2e = _rms(x1e, g2_ref[...])
    zeros = jnp.zeros((HALO, D_MODEL), f32)
    he = jnp.concatenate([jnp.where(i > 0, h2e[0:HALO], zeros), h2e[HALO:HALO + tm],
                          jnp.where(i < n_tiles - 1, h2e[HALO + tm:], zeros)],
                         axis=0).astype(jnp.bfloat16)
    n_chunks = D_FF // FF_CHUNK

    def cols(ref, j):
        g0 = j * FF_CHUNK
        return jnp.concatenate([ref[:, g0:g0 + FF_CHUNK], ref[:, D_FF + g0:D_FF + g0 + FF_CHUNK]], axis=1)

    def up(j):
        return jnp.dot(he, cols(wup_ref, j), preferred_element_type=jnp.float32)

    u_next = up(0)
    for j in range(n_chunks):
        u = u_next
        if j + 1 < n_chunks:
            u_next = up(j + 1)
        cw = cols(cw_ref, j)
        cb = cols(cb_ref, j)
        y = (u[HALO - 1:HALO - 1 + tm] * cw[0:1] + u[HALO:HALO + tm] * cw[1:2]
             + u[HALO + 1:HALO + 1 + tm] * cw[2:3] + cb)
        gate = y[:, 0:FF_CHUNK]
        val = y[:, FF_CHUNK:2 * FF_CHUNK]
        act_ref[:, j * FF_CHUNK:(j + 1) * FF_CHUNK] = (jax.nn.silu(gate) * val).astype(jnp.bfloat16)
    x2 = x1 + jnp.dot(act_ref[...], wdn_ref[...], preferred_element_type=jnp.float32)
    o_ref[0] = _rms(x2, gf_ref[...]) if final_norm else x2


def _mix_ffn(x, oa, ob, gates, w_out, norm2_g, wup, cw, cb, wdn, final_g, final_norm):
    B, S, _ = x.shape
    tm = TM_FFN
    n_tiles = S // tm
    cur = lambda b, i: (b, i, 0)

    def halo_specs(width, rows):
        hb = tm // rows
        prev = lambda b, i: (b, jnp.maximum(i * hb - 1, 0), 0)
        nxt = lambda b, i: (b, jnp.minimum((i + 1) * hb, S // rows - 1), 0)
        return [pl.BlockSpec((1, rows, width), prev), pl.BlockSpec((1, tm, width), cur),
                pl.BlockSpec((1, rows, width), nxt)]

    return pl.pallas_call(
        functools.partial(_mix_ffn_kernel, n_tiles=n_tiles, final_norm=final_norm),
        grid=(B, n_tiles),
        in_specs=(halo_specs(D_MODEL, HALO) + halo_specs(D_MODEL, BF16_ROWS) + halo_specs(D_MODEL, BF16_ROWS)
                  + halo_specs(2 * D_MODEL, BF16_ROWS)
                  + [_const_spec(w_out.shape), _const_spec((1, D_MODEL)), _const_spec(wup.shape),
                     _const_spec(cw.shape), _const_spec(cb.shape), _const_spec(wdn.shape),
                     _const_spec((1, D_MODEL))]),
        out_specs=pl.BlockSpec((1, tm, D_MODEL), cur),
        out_shape=jax.ShapeDtypeStruct((B, S, D_MODEL), jnp.float32),
        scratch_shapes=[pltpu.VMEM((tm, D_FF), jnp.bfloat16)],
        compiler_params=_params(("parallel", "arbitrary")),
        name="mix_ffn",
    )(x, x, x, oa, oa, oa, ob, ob, ob, gates, gates, gates, w_out, norm2_g, wup, cw, cb, wdn, final_g)


def _prep_w_in(w):
    w = w.astype(jnp.bfloat16)
    pad = jnp.zeros((D_MODEL, LANES - QK_ROPE), w.dtype)
    split = Q_LORA + KV_LORA + QK_ROPE
    return jnp.concatenate([w[:, :split], pad, w[:, split:]], axis=1)


def _prep_w_q_b(w):
    w3 = w.reshape(Q_LORA, H_A, QK_NOPE + QK_ROPE)
    pad = jnp.zeros((Q_LORA, H_A, QK_PAD - QK_NOPE - QK_ROPE), w.dtype)
    return jnp.concatenate([w3, pad], axis=-1).reshape(Q_LORA, H_A * QK_PAD).astype(jnp.bfloat16)


def _prep_w_kv_b(w):
    w3 = w.reshape(KV_LORA, H_A, QK_NOPE + V_DIM)
    k = w3[:, :, :QK_NOPE].reshape(KV_LORA, H_A * QK_NOPE)
    v = w3[:, :, QK_NOPE:].reshape(KV_LORA, H_A * V_DIM)
    return jnp.concatenate([k, v], axis=1).astype(jnp.bfloat16)


def kernel(x, positions, norm1_g, w_in, q_a_norm_g, w_q_b, kv_a_norm_g, w_kv_b, rel_bias, sinks,
           w_out, norm2_g, w_up, conv_w, conv_b, w_down, final_norm_g):
    B, S, D = x.shape
    depth = norm1_g.shape[0]
    cos, sin = _rope_tables(positions)
    bias = _window_bias(rel_bias)
    f32 = jnp.float32
    for l in range(depth):
        x2d = x.reshape(B * S, D)
        qa, ka, vat, qb, kb_rep, vbt, gates = _input_projection(
            x2d, norm1_g[l][None].astype(f32), _prep_w_in(w_in[l]),
            q_a_norm_g[l][None].astype(f32), _prep_w_q_b(w_q_b[l]),
            kv_a_norm_g[l][None].astype(f32), _prep_w_kv_b(w_kv_b[l]), cos, sin, S)
        r3 = lambda a: a.reshape(B, S, a.shape[-1])
        oa = _mla_attention(r3(qa), r3(ka), vat)
        sink_rows = jnp.repeat(sinks[l].astype(f32) * LOG2E, Q_BLOCK).reshape(KV_B, 1, GROUP * Q_BLOCK)
        ob = _window_attention(r3(qb), r3(kb_rep), vbt, bias, sink_rows)
        x = _mix_ffn(x, oa, ob, r3(gates), w_out[l].astype(jnp.bfloat16), norm2_g[l][None].astype(f32),
                     w_up[l].astype(jnp.bfloat16), conv_w[l].astype(f32), conv_b[l][None].astype(f32),
                     w_down[l].astype(jnp.bfloat16), final_norm_g[None].astype(f32),
                     final_norm=(l == depth - 1))
    return x
```
